```python
import math
import jax, jax.numpy as jnp
from jax import lax
import numpy as np

D_MODEL = 1024
BATCH = 4
SEQ = 8192
DEPTH = 2
DEC_BATCH = 32
DEC_SEQ = 2048
PAST_LEN = 128

GRID_W = 64
HEAD_DIM = 64
N_HEADS = 8
N_KV = 2
Q_BLOCK = 128
ROPE_THETA = 10000.0
AXIS_DIM = HEAD_DIM // 2
N_FREQ = AXIS_DIM // 2
H_RET = 4
RET_DK = 64
RET_DV = 128
RET_CHUNK = 128
D_FF = 2816
CONV_W = 3
EPS = 1e-6

W_QA = N_HEADS * HEAD_DIM
W_KA = N_KV * HEAD_DIM
W_VA = N_KV * HEAD_DIM
W_QR = H_RET * RET_DK
W_KR = H_RET * RET_DK
W_VR = H_RET * RET_DV
W_GR = H_RET * RET_DV
W_GATE = 2 * D_MODEL
IN_COLS = W_QA + W_KA + W_VA + W_QR + W_KR + W_VR + W_GR + W_GATE
SPLITS = [W_QA,
          W_QA + W_KA,
          W_QA + W_KA + W_VA,
          W_QA + W_KA + W_VA + W_QR,
          W_QA + W_KA + W_VA + W_QR + W_KR,
          W_QA + W_KA + W_VA + W_QR + W_KR + W_VR,
          W_QA + W_KA + W_VA + W_QR + W_KR + W_VR + W_GR]

kernel_name = "hybrid_gqa_retention_convffn_encoder"


def rms_norm(x, g):
    xf = x.astype(jnp.float32)
    y = xf * lax.rsqrt(jnp.mean(xf * xf, axis=-1, keepdims=True) + EPS)
    return (y * g.astype(jnp.float32)).astype(x.dtype)


def axial_rope_tables(n_tokens):
    rows = n_tokens // GRID_W
    row = jnp.repeat(jnp.arange(rows, dtype=jnp.float32), GRID_W)
    col = jnp.tile(jnp.arange(GRID_W, dtype=jnp.float32), rows)
    freqs = 1.0 / (ROPE_THETA ** (jnp.arange(N_FREQ, dtype=jnp.float32) / N_FREQ))
    ang = jnp.stack([row[:, None] * freqs, col[:, None] * freqs], axis=1)
    return jnp.cos(ang), jnp.sin(ang)


def apply_rope(x, cos, sin):
    shp = x.shape
    xs = x.reshape(shp[:-1] + (2, 2, N_FREQ))
    x1 = xs[..., 0, :]
    x2 = xs[..., 1, :]
    c = cos[:, None].astype(x.dtype)
    s = sin[:, None].astype(x.dtype)
    out = jnp.stack([x1 * c - x2 * s, x2 * c + x1 * s], axis=-2)
    return out.reshape(shp)


def gqa_blocked(q, k, v):
    B, S, H, dh = q.shape
    G = H // N_KV
    nb = S // Q_BLOCK
    qb = q.reshape(B, nb, Q_BLOCK, N_KV, G, dh).transpose(1, 0, 3, 4, 2, 5)
    kt = k.transpose(0, 2, 1, 3)
    vt = v.transpose(0, 2, 1, 3)
    scale = 1.0 / math.sqrt(dh)

    def one_block(qblk):
        s = jnp.einsum('bkgqd,bksd->bkgqs', qblk, kt).astype(jnp.float32) * scale
        p = jax.nn.softmax(s, axis=-1).astype(vt.dtype)
        return jnp.einsum('bkgqs,bksd->bkgqd', p, vt)

    o = lax.map(one_block, qb)
    return o.transpose(1, 0, 4, 2, 3, 5).reshape(B, S, H * dh)


def retention_dir(q, k, v, log_g, strict):
    B, H, S, dk = q.shape
    dv = v.shape[-1]
    nc = S // RET_CHUNK
    C = RET_CHUNK
    qc = q.reshape(B, H, nc, C, dk)
    kc = k.reshape(B, H, nc, C, dk)
    vc = v.reshape(B, H, nc, C, dv)
    pos = jnp.arange(C, dtype=jnp.float32)
    diff = pos[:, None] - pos[None, :]
    mask = (diff > 0) if strict else (diff >= 0)
    dmat = jnp.where(mask, jnp.exp(log_g[:, None, None] * jnp.maximum(diff, 0.0)), 0.0)
    s = jnp.einsum('bhncd,bhnmd->bhncm', qc, kc) * dmat[None, :, None]
    inner = jnp.einsum('bhncm,bhnme->bhnce', s, vc)
    kdec = jnp.exp(log_g[:, None] * (C - 1 - pos))
    kv = jnp.einsum('bhnmd,bhnme,hm->nbhde', kc, vc, kdec)
    chunk_decay = jnp.exp(log_g * C)[:, None, None]

    def step(R, kv_n):
        return chunk_decay * R + kv_n, R

    _, r_prev = lax.scan(step, jnp.zeros((B, H, dk, dv), jnp.float32), kv)
    qdec = jnp.exp(log_g[:, None] * (pos + 1.0))
    cross = jnp.einsum('bhncd,hc,nbhde->bhnce', qc, qdec, r_prev)
    return (inner + cross).reshape(B, H, S, dv)


def depthwise_conv3(u, w, b):
    up = jnp.pad(u, ((0, 0), (1, 1), (0, 0)))
    return up[:, :-2] * w[0] + up[:, 1:-1] * w[1] + up[:, 2:] * w[2] + b


def encoder_layer(x, cos, sin, norm1_g, w_in, q_norm_g, k_norm_g, dec_f, dec_b,
                  w_attn_o, w_ret_o, b_gate, w_out, norm2_g, w_up, conv_w, conv_b, w_down):
    B, S, _ = x.shape
    h = rms_norm(x, norm1_g)
    proj = h @ w_in
    qa, ka, va, qr, kr, vr, gr, gate_logits = jnp.split(proj, SPLITS, axis=-1)

    qa = apply_rope(rms_norm(qa.reshape(B, S, N_HEADS, HEAD_DIM), q_norm_g), cos, sin)
    ka = apply_rope(rms_norm(ka.reshape(B, S, N_KV, HEAD_DIM), k_norm_g), cos, sin)
    va = va.reshape(B, S, N_KV, HEAD_DIM)
    attn = gqa_blocked(qa, ka, va)
    attn_d = attn @ w_attn_o

    qr = apply_rope(qr.reshape(B, S, H_RET, RET_DK), cos, sin)
    kr = apply_rope(kr.reshape(B, S, H_RET, RET_DK), cos, sin) * (RET_DK ** -0.5)
    q_t = qr.transpose(0, 2, 1, 3).astype(jnp.float32)
    k_t = kr.transpose(0, 2, 1, 3).astype(jnp.float32)
    v_t = vr.reshape(B, S, H_RET, RET_DV).transpose(0, 2, 1, 3).astype(jnp.float32)
    lf = jax.nn.log_sigmoid(dec_f.astype(jnp.float32))
    lb = jax.nn.log_sigmoid(dec_b.astype(jnp.float32))
    y_f = retention_dir(q_t, k_t, v_t, lf, False)
    y_b = jnp.flip(retention_dir(jnp.flip(q_t, 2), jnp.flip(k_t, 2), jnp.flip(v_t, 2), lb, True), 2)
    y = y_f + y_b
    y = y * lax.rsqrt(jnp.mean(y * y, axis=-1, keepdims=True) + EPS)
    y = y.transpose(0, 2, 1, 3).reshape(B, S, W_VR).astype(x.dtype)
    ret_d = (jax.nn.silu(gr) * y) @ w_ret_o

    gates = jax.nn.sigmoid(gate_logits.astype(jnp.float32) + b_gate.astype(jnp.float32)).astype(x.dtype)
    g_a, g_r = jnp.split(gates, 2, axis=-1)
    x = x + (g_a * attn_d + g_r * ret_d) @ w_out

    h2 = rms_norm(x, norm2_g)
    u = depthwise_conv3(h2 @ w_up, conv_w, conv_b)
    val, gt = jnp.split(u, 2, axis=-1)
    x = x + (jax.nn.gelu(gt, approximate=False) * val) @ w_down
    return x


def setup_inputs(seed: int = 0) -> dict:
    key = jax.random.key(seed)
    ks = jax.random.split(key, 20)
    f32 = jnp.float32

    def nrm(k, shape, scale):
        return jax.random.normal(k, shape, f32) * scale

    base_logit = jnp.log(2.0 ** (5.0 + jnp.arange(H_RET, dtype=f32)) - 1.0)
    return {
        "x_prompt": nrm(ks[0], (BATCH, SEQ, D_MODEL), 1.0),
        "x_sample": nrm(ks[1], (DEC_BATCH, DEC_SEQ, D_MODEL), 1.0),
        "norm1_g": 1.0 + nrm(ks[2], (DEPTH, D_MODEL), 0.01),
        "w_in": nrm(ks[3], (DEPTH, D_MODEL, IN_COLS), D_MODEL ** -0.5),
        "q_norm_g": 1.0 + nrm(ks[4], (DEPTH, HEAD_DIM), 0.01),
        "k_norm_g": 1.0 + nrm(ks[5], (DEPTH, HEAD_DIM), 0.01),
        "ret_decay_fwd": base_logit[None] + nrm(ks[6], (DEPTH, H_RET), 0.1),
        "ret_decay_bwd": base_logit[None] + nrm(ks[7], (DEPTH, H_RET), 0.1),
        "w_attn_o": nrm(ks[8], (DEPTH, W_QA, D_MODEL), W_QA ** -0.5),
        "w_ret_o": nrm(ks[9], (DEPTH, W_VR, D_MODEL), W_VR ** -0.5),
        "b_gate": nrm(ks[10], (DEPTH, W_GATE), 0.1),
        "w_out": nrm(ks[11], (DEPTH, D_MODEL, D_MODEL), D_MODEL ** -0.5),
        "norm2_g": 1.0 + nrm(ks[12], (DEPTH, D_MODEL), 0.01),
        "w_up": nrm(ks[13], (DEPTH, D_MODEL, 2 * D_FF), D_MODEL ** -0.5),
        "conv_w": nrm(ks[14], (DEPTH, CONV_W, 2 * D_FF), CONV_W ** -0.5),
        "conv_b": nrm(ks[15], (DEPTH, 2 * D_FF), 0.01),
        "w_down": nrm(ks[16], (DEPTH, D_FF, D_MODEL), D_FF ** -0.5),
    }


def reference(x_prompt, x_sample, norm1_g, w_in, q_norm_g, k_norm_g, ret_decay_fwd, ret_decay_bwd,
              w_attn_o, w_ret_o, b_gate, w_out, norm2_g, w_up, conv_w, conv_b, w_down):
    cos_p, sin_p = axial_rope_tables(x_prompt.shape[1])
    cos_s, sin_s = axial_rope_tables(x_sample.shape[1])
    y_prompt = x_prompt
    y_sample = x_sample
    for l in range(DEPTH):
        params = (norm1_g[l], w_in[l], q_norm_g[l], k_norm_g[l], ret_decay_fwd[l], ret_decay_bwd[l],
                  w_attn_o[l], w_ret_o[l], b_gate[l], w_out[l], norm2_g[l], w_up[l], conv_w[l],
                  conv_b[l], w_down[l])
        y_prompt = encoder_layer(y_prompt, cos_p, sin_p, *params)
        y_sample = encoder_layer(y_sample, cos_s, sin_s, *params)
    return (y_prompt, y_sample)
```

```python
import functools
import math

import jax
import jax.numpy as jnp
from jax import lax
from jax.experimental import pallas as pl
from jax.experimental.pallas import tpu as pltpu

F32 = jnp.float32
BF16 = jnp.bfloat16

D_MODEL = 1024
GRID_W = 64
HEAD_DIM = 64
N_HEADS = 8
N_KV = 2
GROUP = N_HEADS // N_KV
ROPE_THETA = 10000.0
N_FREQ = HEAD_DIM // 4
H_RET = 4
RET_DK = 64
RET_DV = 128
RET_CHUNK = 128
D_FF = 2816
EPS = 1e-6

W_QA = N_HEADS * HEAD_DIM
W_KA = N_KV * HEAD_DIM
W_VA = N_KV * HEAD_DIM
W_QR = H_RET * RET_DK
W_KR = H_RET * RET_DK
W_VR = H_RET * RET_DV
W_GR = H_RET * RET_DV
W_GATE = 2 * D_MODEL
OFF_KA = W_QA
OFF_VA = OFF_KA + W_KA
OFF_QR = OFF_VA + W_VA
OFF_KR = OFF_QR + W_QR
OFF_VR = OFF_KR + W_KR
OFF_GR = OFF_VR + W_VR
OFF_GATE = OFF_GR + W_GR
IN_COLS = OFF_GATE + W_GATE

LANES = 128
BF16_ROWS = 16
LOG2E = 1.4426950408889634

TM_PROJ = 512
TQ = 128
TKV = 256
VT_ROWS = HEAD_DIM + BF16_ROWS
TS_RET = 512
FC = 256
HALO = 8
VMEM_LIMIT = 56 * 1024 * 1024


def _cparams(n_axes):
    return pltpu.CompilerParams(dimension_semantics=("arbitrary",) * n_axes, vmem_limit_bytes=VMEM_LIMIT)


def _const_spec(shape):
    nd = len(shape)
    return pl.BlockSpec(shape, lambda *_: (0,) * nd, pipeline_mode=pl.Buffered(1))


def _rope_tables(seq_len):
    t = jnp.arange(seq_len, dtype=F32)
    row = jnp.floor(t / GRID_W)
    col = t - row * GRID_W
    freqs = 1.0 / (ROPE_THETA ** (jnp.arange(N_FREQ, dtype=F32) / N_FREQ))
    ang = jnp.stack([row[:, None] * freqs, col[:, None] * freqs], axis=1)
    cos = jnp.cos(ang)
    sin = jnp.sin(ang)
    zero = jnp.zeros_like(sin)
    cos_h = jnp.stack([cos, cos], axis=2).reshape(seq_len, HEAD_DIM)
    sa_h = jnp.stack([-sin, zero], axis=2).reshape(seq_len, HEAD_DIM)
    sb_h = jnp.stack([zero, sin], axis=2).reshape(seq_len, HEAD_DIM)
    tile2 = lambda a: jnp.concatenate([a, a], axis=1)
    return tile2(cos_h), tile2(sa_h), tile2(sb_h)


def _rope(x, cos, sa, sb):
    return x * cos + pltpu.roll(x, LANES - N_FREQ, 1) * sa + pltpu.roll(x, N_FREQ, 1) * sb


def _in_proj_kernel(x_ref, g1_ref, w_ref, qg_ref, kg_ref, bg_ref, cos_ref, sa_ref, sb_ref, gm_ref,
                    q_ref, k_ref, vt_ref, qr_ref, kr_ref, vr_ref, gr_ref, gate_ref):
    tm = x_ref.shape[0]
    x = x_ref[...]
    h = (x * lax.rsqrt(jnp.mean(x * x, axis=-1, keepdims=True) + EPS) * g1_ref[...]).astype(BF16)
    cos = cos_ref[...]
    sa = sa_ref[...]
    sb = sb_ref[...]
    lane = lax.broadcasted_iota(jnp.int32, (tm, LANES), 1)
    low = lane < HEAD_DIM

    def proj(c0, width):
        return jnp.dot(h, w_ref[:, c0:c0 + width], preferred_element_type=F32)

    def head_rms(blk):
        ss = blk * blk
        hi = ss.astype(BF16)
        lo = (ss - hi.astype(F32)).astype(BF16)
        ms = jnp.dot(jnp.concatenate([hi, lo], axis=1), gm_ref[...], preferred_element_type=F32)
        return lax.rsqrt(ms + EPS)

    qa = proj(0, W_QA)
    for b in range(GROUP):
        blk = qa[:, b * LANES:(b + 1) * LANES]
        out = _rope(blk * head_rms(blk) * qg_ref[...], cos, sa, sb)
        q_ref[0, b] = jnp.where(low, out, 0.0).astype(BF16)
        q_ref[1, b] = jnp.where(low, 0.0, out).astype(BF16)

    kv = proj(OFF_KA, W_KA + W_VA)
    kblk = kv[:, :LANES]
    k_ref[...] = _rope(kblk * head_rms(kblk) * kg_ref[...], cos, sa, sb).astype(BF16)
    vt = kv[:, LANES:].T
    ones_rows = (lax.broadcasted_iota(jnp.int32, (BF16_ROWS, TKV), 0) == 0).astype(BF16)
    for g in range(N_KV):
        for c in range(tm // TKV):
            vt_ref[g, c, :HEAD_DIM, :] = vt[g * HEAD_DIM:(g + 1) * HEAD_DIM, c * TKV:(c + 1) * TKV].astype(BF16)
            vt_ref[g, c, HEAD_DIM:, :] = ones_rows

    qr = proj(OFF_QR, W_QR)
    kr = proj(OFF_KR, W_KR)
    for b in range(H_RET // 2):
        qo = _rope(qr[:, b * LANES:(b + 1) * LANES], cos, sa, sb)
        ko = _rope(kr[:, b * LANES:(b + 1) * LANES], cos, sa, sb) * (RET_DK ** -0.5)
        qr_ref[2 * b] = jnp.where(low, qo, 0.0).astype(BF16)
        qr_ref[2 * b + 1] = jnp.where(low, 0.0, qo).astype(BF16)
        kr_ref[2 * b] = jnp.where(low, ko, 0.0).astype(BF16)
        kr_ref[2 * b + 1] = jnp.where(low, 0.0, ko).astype(BF16)

    vr_ref[...] = proj(OFF_VR, W_VR).astype(BF16)
    gr = proj(OFF_GR, W_GR)
    gr_ref[...] = (gr * jax.nn.sigmoid(gr)).astype(BF16)
    for c in range(W_GATE // 512):
        z = proj(OFF_GATE + c * 512, 512) + bg_ref[:, c * 512:(c + 1) * 512]
        gate_ref[:, c * 512:(c + 1) * 512] = jax.nn.sigmoid(z).astype(BF16)


def _in_proj(x2, seq_len, norm_g, w_in_p, qg, kg, b_gate, tables, gmat):
    n = x2.shape[0]
    tm = TM_PROJ
    nt = seq_len // tm
    cos, sa, sb = tables
    tab_spec = pl.BlockSpec((tm, LANES), lambda i: (i % nt, 0))
    row = lambda width: pl.BlockSpec((tm, width), lambda i: (i, 0))
    heads = lambda nh: pl.BlockSpec((nh, tm, LANES), lambda i: (0, i, 0))
    n_b = n // seq_len
    return pl.pallas_call(
        _in_proj_kernel,
        grid=(n // tm,),
        in_specs=[row(D_MODEL), _const_spec((1, D_MODEL)), _const_spec((D_MODEL, IN_COLS)),
                  _const_spec((1, LANES)), _const_spec((1, LANES)), _const_spec((1, W_GATE)),
                  tab_spec, tab_spec, tab_spec, _const_spec((2 * LANES, LANES))],
        out_specs=[pl.BlockSpec((N_KV, GROUP, tm, LANES), lambda i: (0, 0, i, 0)),
                   row(LANES),
                   pl.BlockSpec((None, N_KV, tm // TKV, VT_ROWS, TKV), lambda i: (i // nt, 0, i % nt, 0, 0)),
                   heads(H_RET), heads(H_RET), row(W_VR), row(W_GR), row(W_GATE)],
        out_shape=[jax.ShapeDtypeStruct((N_KV, GROUP, n, LANES), BF16),
                   jax.ShapeDtypeStruct((n, LANES), BF16),
                   jax.ShapeDtypeStruct((n_b, N_KV, seq_len // TKV, VT_ROWS, TKV), BF16),
                   jax.ShapeDtypeStruct((H_RET, n, LANES), BF16),
                   jax.ShapeDtypeStruct((H_RET, n, LANES), BF16),
                   jax.ShapeDtypeStruct((n, W_VR), BF16),
                   jax.ShapeDtypeStruct((n, W_GR), BF16),
                   jax.ShapeDtypeStruct((n, W_GATE), BF16)],
        compiler_params=_cparams(1),
        name="in_proj",
    )(x2, norm_g, w_in_p, qg, kg, b_gate, cos, sa, sb, gmat)


def _attn_kernel(q_ref, k_ref, vt_ref, o_ref, s_scr, p_scr, acc_scr, m_scr, *, n_kv):
    q = q_ref[...].reshape(GROUP * TQ, LANES)
    dn = (((1,), (1,)), ((), ()))

    def scores(j):
        off = pl.multiple_of(j * TKV, TKV)
        return lax.dot_general(k_ref[pl.ds(off, TKV), :], q, dn, preferred_element_type=F32)

    s_scr[...] = scores(0)
    p_scr[...] = jnp.zeros_like(p_scr)
    acc_scr[...] = jnp.zeros_like(acc_scr)
    m_scr[...] = jnp.full_like(m_scr, -1e30)

    def body(j, alpha_prev):
        s_next = scores(jnp.minimum(j + 1, n_kv - 1))
        s = s_scr[...]
        m_old = m_scr[...]
        m_new = jnp.maximum(m_old, jnp.max(s, axis=0, keepdims=True))
        p = jnp.exp2(s - m_new).astype(BF16)
        alpha = jnp.exp2(m_old - m_new)
        pv = jnp.dot(vt_ref[jnp.maximum(j - 1, 0)], p_scr[...], preferred_element_type=F32)
        acc_scr[...] = acc_scr[...] * alpha_prev + pv
        p_scr[...] = p
        s_scr[...] = s_next
        m_scr[...] = m_new
        return alpha

    alpha_last = lax.fori_loop(0, n_kv, body, jnp.ones((1, GROUP * TQ), F32))
    acc = acc_scr[...] * alpha_last + jnp.dot(vt_ref[n_kv - 1], p_scr[...], preferred_element_type=F32)
    o = acc[:HEAD_DIM] / acc[HEAD_DIM:HEAD_DIM + 1]
    z = jnp.concatenate([o[:, hh * TQ:(hh + 1) * TQ] for hh in range(GROUP)], axis=0)
    o_ref[...] = z.T.astype(o_ref.dtype)


def _attention(q, k, vt, seq_len):
    n = k.shape[0]
    n_b = n // seq_len
    nq = seq_len // TQ
    n_kv = seq_len // TKV
    return pl.pallas_call(
        functools.partial(_attn_kernel, n_kv=n_kv),
        grid=(n_b, N_KV, nq),
        in_specs=[pl.BlockSpec((None, GROUP, TQ, LANES), lambda b, g, i: (g, 0, b * nq + i, 0)),
                  pl.BlockSpec((seq_len, LANES), lambda b, g, i: (b, 0)),
                  pl.BlockSpec((None, None, n_kv, VT_ROWS, TKV), lambda b, g, i: (b, g, 0, 0, 0))],
        out_specs=pl.BlockSpec((TQ, GROUP * HEAD_DIM), lambda b, g, i: (b * nq + i, g)),
        out_shape=jax.ShapeDtypeStruct((n, W_QA), BF16),
        scratch_shapes=[pltpu.VMEM((TKV, GROUP * TQ), F32), pltpu.VMEM((TKV, GROUP * TQ), BF16),
                        pltpu.VMEM((VT_ROWS, GROUP * TQ), F32), pltpu.VMEM((1, GROUP * TQ), F32)],
        compiler_params=_cparams(3),
        name="attn",
    )(q, k, vt)


def _decay_tables(dec_ref, h):
    lg = jax.nn.log_sigmoid(dec_ref[h])
    return jnp.broadcast_to(lg[0:1, :], (RET_CHUNK, LANES))


def _ret_fwd_kernel(q_ref, k_ref, v_ref, decf_ref, decb_ref, y_ref, r_scr):
    ts = v_ref.shape[0]
    c_len = RET_CHUNK

    @pl.when(pl.program_id(1) == 0)
    def _():
        r_scr[...] = jnp.zeros_like(r_scr)

    ri = lax.broadcasted_iota(jnp.int32, (c_len, LANES), 0).astype(F32)
    ci = lax.broadcasted_iota(jnp.int32, (c_len, LANES), 1).astype(F32)
    diff = ri - ci
    dn_t = (((1,), (1,)), ((), ()))
    dn_kv = (((0,), (0,)), ((), ()))
    for h in range(H_RET):
        lf = _decay_tables(decf_ref, h)
        lb = _decay_tables(decb_ref, h)
        dmat = jnp.where(diff >= 0, jnp.exp(lf * jnp.maximum(diff, 0.0)), jnp.exp(lb * jnp.maximum(-diff, 0.0)))
        qdec = jnp.exp(lf * (ri + 1.0))
        kdec = jnp.exp(lf * (c_len - 1.0 - ri))
        cdec = jnp.exp(lf * float(c_len))
        for c in range(ts // c_len):
            rows = slice(c * c_len, (c + 1) * c_len)
            qh = q_ref[h, rows, :]
            kh = k_ref[h, rows, :]
            vh = v_ref[rows, h * RET_DV:(h + 1) * RET_DV]
            s = lax.dot_general(qh, kh, dn_t, preferred_element_type=F32)
            inner = jnp.dot((s * dmat).astype(BF16), vh, preferred_element_type=F32)
            r = r_scr[h]
            cross = jnp.dot(qh, r.astype(BF16), preferred_element_type=F32) * qdec
            y_ref[rows, h * RET_DV:(h + 1) * RET_DV] = inner + cross
            vd = (vh.astype(F32) * kdec).astype(BF16)
            r_scr[h] = r * cdec + lax.dot_general(kh, vd, dn_kv, preferred_element_type=F32)


def _ret_bwd_kernel(q_ref, k_ref, v_ref, y_ref, gr_ref, decb_ref, o_ref, r_scr):
    ts = v_ref.shape[0]
    c_len = RET_CHUNK

    @pl.when(pl.program_id(1) == 0)
    def _():
        r_scr[...] = jnp.zeros_like(r_scr)

    ri = lax.broadcasted_iota(jnp.int32, (c_len, LANES), 0).astype(F32)
    dn_kv = (((0,), (0,)), ((), ()))
    for h in range(H_RET):
        lb = _decay_tables(decb_ref, h)
        qdec = jnp.exp(lb * (float(c_len) - ri))
        kdec = jnp.exp(lb * ri)
        cdec = jnp.exp(lb * float(c_len))
        for c in reversed(range(ts // c_len)):
            rows = slice(c * c_len, (c + 1) * c_len)
            cols = slice(h * RET_DV, (h + 1) * RET_DV)
            qh = q_ref[h, rows, :]
            kh = k_ref[h, rows, :]
            vh = v_ref[rows, cols]
            r = r_scr[h]
            y = y_ref[rows, cols] + jnp.dot(qh, r.astype(BF16), preferred_element_type=F32) * qdec
            yn = y * lax.rsqrt(jnp.mean(y * y, axis=-1, keepdims=True) + EPS)
            o_ref[rows, cols] = (gr_ref[rows, cols].astype(F32) * yn).astype(o_ref.dtype)
            vd = (vh.astype(F32) * kdec).astype(BF16)
            r_scr[h] = r * cdec + lax.dot_general(kh, vd, dn_kv, preferred_element_type=F32)


def _retention(qr, kr, vr, gr, decf, decb, seq_len):
    n = vr.shape[0]
    n_b = n // seq_len
    ts = TS_RET
    nt = seq_len // ts
    fwd = lambda b, j: b * nt + j
    bwd = lambda b, j: b * nt + (nt - 1 - j)
    heads = lambda f: pl.BlockSpec((H_RET, ts, LANES), lambda b, j: (0, f(b, j), 0))
    row = lambda f: pl.BlockSpec((ts, W_VR), lambda b, j: (f(b, j), 0))
    dec_spec = _const_spec((H_RET, 8, LANES))
    state = pltpu.VMEM((H_RET, LANES, RET_DV), F32)
    y1 = pl.pallas_call(
        _ret_fwd_kernel,
        grid=(n_b, nt),
        in_specs=[heads(fwd), heads(fwd), row(fwd), dec_spec, dec_spec],
        out_specs=row(fwd),
        out_shape=jax.ShapeDtypeStruct((n, W_VR), F32),
        scratch_shapes=[state],
        compiler_params=_cparams(2),
        name="ret_fwd",
    )(qr, kr, vr, decf, decb)
    return pl.pallas_call(
        _ret_bwd_kernel,
        grid=(n_b, nt),
        in_specs=[heads(bwd), heads(bwd), row(bwd), row(bwd), row(bwd), dec_spec],
        out_specs=row(bwd),
        out_shape=jax.ShapeDtypeStruct((n, W_VR), BF16),
        scratch_shapes=[state],
        compiler_params=_cparams(2),
        name="ret_bwd",
    )(qr, kr, vr, y1, gr, decb)


def _merge_kernel(x_ref, a_ref, r_ref, gate_ref, wa_ref, wr_ref, wo_ref, o_ref):
    ad = jnp.dot(a_ref[...], wa_ref[...], preferred_element_type=F32)
    rd = jnp.dot(r_ref[...], wr_ref[...], preferred_element_type=F32)
    g_a = gate_ref[:, :D_MODEL].astype(F32)
    g_r = gate_ref[:, D_MODEL:].astype(F32)
    m = (g_a * ad + g_r * rd).astype(BF16)
    o_ref[...] = x_ref[...] + jnp.dot(m, wo_ref[...], preferred_element_type=F32)


def _merge(x2, attn, ret, gates, wa, wr, wo):
    n = x2.shape[0]
    tm = TM_PROJ
    row = lambda width: pl.BlockSpec((tm, width), lambda i: (i, 0))
    return pl.pallas_call(
        _merge_kernel,
        grid=(n // tm,),
        in_specs=[row(D_MODEL), row(W_QA), row(W_VR), row(W_GATE),
                  _const_spec((W_QA, D_MODEL)), _const_spec((W_VR, D_MODEL)), _const_spec((D_MODEL, D_MODEL))],
        out_specs=row(D_MODEL),
        out_shape=jax.ShapeDtypeStruct((n, D_MODEL), F32),
        compiler_params=_cparams(1),
        name="merge",
    )(x2, attn, ret, gates, wa, wr, wo)


def _ffn_kernel(xp_ref, x_ref, xn_ref, g2_ref, wup_ref, cw_ref, cb_ref, wdn_ref, o_ref, *, seq_len):
    tm = x_ref.shape[0]
    i = pl.program_id(0)
    first = (i * tm) % seq_len == 0
    last = ((i + 1) * tm) % seq_len == 0

    def norm(x):
        return x * lax.rsqrt(jnp.mean(x * x, axis=-1, keepdims=True) + EPS) * g2_ref[...]

    hp = jnp.where(first, 0.0, norm(xp_ref[...]))
    hn = jnp.where(last, 0.0, norm(xn_ref[...]))
    x = x_ref[...]
    h = jnp.concatenate([hp, norm(x), hn], axis=0).astype(BF16)

    def conv(u, c0):
        w = cw_ref[:, c0:c0 + FC]
        return (u[HALO - 1:HALO - 1 + tm] * w[0:1] + u[HALO:HALO + tm] * w[1:2]
                + u[HALO + 1:HALO + 1 + tm] * w[2:3] + cb_ref[:, c0:c0 + FC])

    acc = jnp.zeros((tm, D_MODEL), F32)
    for f in range(D_FF // FC):
        c0 = f * FC
        val = conv(jnp.dot(h, wup_ref[:, c0:c0 + FC], preferred_element_type=F32), c0)
        gt = conv(jnp.dot(h, wup_ref[:, D_FF + c0:D_FF + c0 + FC], preferred_element_type=F32), D_FF + c0)
        gelu = 0.5 * gt * (1.0 + lax.erf(gt * (1.0 / math.sqrt(2.0))))
        act = (gelu * val).astype(BF16)
        acc = acc + jnp.dot(act, wdn_ref[c0:c0 + FC, :], preferred_element_type=F32)
    o_ref[...] = x + acc


def _ffn(x2, seq_len, norm_g, w_up, conv_w, conv_b, w_down):
    n = x2.shape[0]
    tm = TM_PROJ
    hb = tm // HALO
    n_hb = n // HALO
    return pl.pallas_call(
        functools.partial(_ffn_kernel, seq_len=seq_len),
        grid=(n // tm,),
        in_specs=[pl.BlockSpec((HALO, D_MODEL), lambda i: (jnp.maximum(i * hb - 1, 0), 0)),
                  pl.BlockSpec((tm, D_MODEL), lambda i: (i, 0)),
                  pl.BlockSpec((HALO, D_MODEL), lambda i: (jnp.minimum((i + 1) * hb, n_hb - 1), 0)),
                  _const_spec((1, D_MODEL)), _const_spec((D_MODEL, 2 * D_FF)),
                  _const_spec((3, 2 * D_FF)), _const_spec((1, 2 * D_FF)), _const_spec((D_FF, D_MODEL))],
        out_specs=pl.BlockSpec((tm, D_MODEL), lambda i: (i, 0)),
        out_shape=jax.ShapeDtypeStruct((n, D_MODEL), F32),
        compiler_params=_cparams(1),
        name="ffn",
    )(x2, x2, x2, norm_g, w_up, conv_w, conv_b, w_down)


def _layer(x2, seq_len, tables, gmat, p):
    q, k, vt, qr, kr, vr, gr, gates = _in_proj(x2, seq_len, p["norm1_g"], p["w_in"], p["qg"], p["kg"],
                                               p["b_gate"], tables, gmat)
    attn = _attention(q, k, vt, seq_len)
    ret = _retention(qr, kr, vr, gr, p["decf"], p["decb"], seq_len)
    x2 = _merge(x2, attn, ret, gates, p["w_attn_o"], p["w_ret_o"], p["w_out"])
    return _ffn(x2, seq_len, p["norm2_g"], p["w_up"], p["conv_w"], p["conv_b"], p["w_down"])


def _layer_params(l, norm1_g, w_in, q_norm_g, k_norm_g, ret_decay_fwd, ret_decay_bwd, w_attn_o, w_ret_o,
                  b_gate, w_out, norm2_g, w_up, conv_w, conv_b, w_down):
    head_order = [hh for b in range(GROUP) for hh in (b, GROUP + b)]
    q_cols = jnp.asarray([hh * HEAD_DIM + d for hh in head_order for d in range(HEAD_DIM)], jnp.int32)
    w = w_in[l]
    w_p = jnp.concatenate([jnp.take(w[:, :W_QA], q_cols, axis=1), w[:, W_QA:]], axis=1).astype(BF16)
    two = lambda g: jnp.concatenate([g, g]).reshape(1, LANES)
    bcast = lambda d: jnp.broadcast_to(d.astype(F32)[:, None, None], (H_RET, 8, LANES))
    return dict(
        norm1_g=norm1_g[l].reshape(1, D_MODEL), w_in=w_p,
        qg=two(q_norm_g[l]) * (LOG2E / math.sqrt(HEAD_DIM)), kg=two(k_norm_g[l]),
        decf=bcast(ret_decay_fwd[l]), decb=bcast(ret_decay_bwd[l]),
        w_attn_o=w_attn_o[l].astype(BF16), w_ret_o=w_ret_o[l].astype(BF16),
        b_gate=b_gate[l].reshape(1, W_GATE), w_out=w_out[l].astype(BF16),
        norm2_g=norm2_g[l].reshape(1, D_MODEL), w_up=w_up[l].astype(BF16),
        conv_w=conv_w[l], conv_b=conv_b[l].reshape(1, 2 * D_FF), w_down=w_down[l].astype(BF16))


def kernel(x_prompt, x_sample, norm1_g, w_in, q_norm_g, k_norm_g, ret_decay_fwd, ret_decay_bwd, w_attn_o, w_ret_o, b_gate, w_out, norm2_g, w_up, conv_w, conv_b, w_down):
    weights = (norm1_g, w_in, q_norm_g, k_norm_g, ret_decay_fwd, ret_decay_bwd, w_attn_o, w_ret_o,
               b_gate, w_out, norm2_g, w_up, conv_w, conv_b, w_down)
    depth = w_in.shape[0]
    lane = jnp.arange(2 * LANES) % LANES
    gmat = ((lane[:, None] // HEAD_DIM) == (jnp.arange(LANES)[None, :] // HEAD_DIM)).astype(BF16) / HEAD_DIM
    outs = []
    for x in (x_prompt, x_sample):
        n_b, seq_len, _ = x.shape
        tables = _rope_tables(seq_len)
        x2 = x.reshape(n_b * seq_len, D_MODEL)
        for l in range(depth):
            x2 = _layer(x2, seq_len, tables, gmat, _layer_params(l, *weights))
        outs.append(x2.reshape(n_b, seq_len, D_MODEL))
    return tuple(outs)
```

```python
import functools
import math

import jax
import jax.numpy as jnp
from jax import lax
from jax.experimental import pallas as pl
from jax.experimental.pallas import tpu as pltpu

F32 = jnp.float32
BF16 = jnp.bfloat16

D_MODEL = 1024
GRID_W = 64
HEAD_DIM = 64
N_HEADS = 8
N_KV = 2
GROUP = N_HEADS // N_KV
ROPE_THETA = 10000.0
N_FREQ = HEAD_DIM // 4
H_RET = 4
RET_DK = 64
RET_DV = 128
RET_CHUNK = 128
D_FF = 2816
EPS = 1e-6

W_QA = N_HEADS * HEAD_DIM
W_KA = N_KV * HEAD_DIM
W_VA = N_KV * HEAD_DIM
W_QR = H_RET * RET_DK
W_KR = H_RET * RET_DK
W_VR = H_RET * RET_DV
W_GR = H_RET * RET_DV
W_GATE = 2 * D_MODEL
OFF_KA = W_QA
OFF_VA = OFF_KA + W_KA
OFF_QR = OFF_VA + W_VA
OFF_KR = OFF_QR + W_QR
OFF_VR = OFF_KR + W_KR
OFF_GR = OFF_VR + W_VR
OFF_GATE = OFF_GR + W_GR
IN_COLS = OFF_GATE + W_GATE

LANES = 128
SUBLANES = 8
BF16_ROWS = 16
LOG2E = 1.4426950408889634

TM_PROJ = 512
TQ = 256
TKV = 512
ATTN_UNROLL = 4
SAFE_SHIFT = 60.0
VT_ROWS = HEAD_DIM + BF16_ROWS
TS_RET = 512
FC = 256
HALO = 8
VMEM_LIMIT = 56 * 1024 * 1024


def _cparams(n_axes):
    return pltpu.CompilerParams(dimension_semantics=("arbitrary",) * n_axes, vmem_limit_bytes=VMEM_LIMIT)


def _const_spec(shape):
    nd = len(shape)
    return pl.BlockSpec(shape, lambda *_: (0,) * nd, pipeline_mode=pl.Buffered(1))


def _rope_tables(seq_len):
    t = jnp.arange(seq_len, dtype=F32)
    row = jnp.floor(t / GRID_W)
    col = t - row * GRID_W
    freqs = 1.0 / (ROPE_THETA ** (jnp.arange(N_FREQ, dtype=F32) / N_FREQ))
    ang = jnp.stack([row[:, None] * freqs, col[:, None] * freqs], axis=1)
    cos = jnp.cos(ang)
    sin = jnp.sin(ang)
    zero = jnp.zeros_like(sin)
    cos_h = jnp.stack([cos, cos], axis=2).reshape(seq_len, HEAD_DIM)
    sa_h = jnp.stack([-sin, zero], axis=2).reshape(seq_len, HEAD_DIM)
    sb_h = jnp.stack([zero, sin], axis=2).reshape(seq_len, HEAD_DIM)
    tile2 = lambda a: jnp.concatenate([a, a], axis=1)
    return tile2(cos_h), tile2(sa_h), tile2(sb_h)


def _rope(x, cos, sa, sb):
    return x * cos + pltpu.roll(x, LANES - N_FREQ, 1) * sa + pltpu.roll(x, N_FREQ, 1) * sb


def _in_proj_kernel(x_ref, g1_ref, w_ref, qg_ref, kg_ref, bg_ref, cos_ref, sa_ref, sb_ref, gm_ref,
                    q_ref, k_ref, vt_ref, qr_ref, kr_ref, vr_ref, gr_ref, gate_ref):
    tm = x_ref.shape[0]
    x = x_ref[...]
    h = (x * lax.rsqrt(jnp.mean(x * x, axis=-1, keepdims=True) + EPS) * g1_ref[...]).astype(BF16)
    cos = cos_ref[...]
    sa = sa_ref[...]
    sb = sb_ref[...]
    lane = lax.broadcasted_iota(jnp.int32, (tm, LANES), 1)
    low = lane < HEAD_DIM

    def proj(c0, width):
        return jnp.dot(h, w_ref[:, c0:c0 + width], preferred_element_type=F32)

    def head_rms(blk):
        ss = blk * blk
        hi = ss.astype(BF16)
        lo = (ss - hi.astype(F32)).astype(BF16)
        ms = jnp.dot(jnp.concatenate([hi, lo], axis=1), gm_ref[...], preferred_element_type=F32)
        return lax.rsqrt(ms + EPS)

    qa = proj(0, W_QA)
    zeros_half = jnp.zeros((HEAD_DIM, tm), BF16)
    for b in range(GROUP):
        blk = qa[:, b * LANES:(b + 1) * LANES]
        out_t = _rope(blk * head_rms(blk) * qg_ref[...], cos, sa, sb).T
        for g in range(N_KV):
            q_ref[g, b, :HEAD_DIM, :] = out_t[g * HEAD_DIM:(g + 1) * HEAD_DIM].astype(BF16)
            q_ref[g, b, HEAD_DIM:, :] = zeros_half

    kv = proj(OFF_KA, W_KA + W_VA)
    kblk = kv[:, :LANES]
    kout = _rope(kblk * head_rms(kblk) * kg_ref[...], cos, sa, sb)
    one_lane = jnp.where(lane == HEAD_DIM, 1.0, 0.0)
    k_ref[0] = jnp.where(low, kout, one_lane).astype(BF16)
    k_ref[1] = jnp.where(low, pltpu.roll(kout, HEAD_DIM, 1), one_lane).astype(BF16)
    vt = kv[:, LANES:].T
    ones_rows = (lax.broadcasted_iota(jnp.int32, (BF16_ROWS, TKV), 0) == 0).astype(BF16)
    for g in range(N_KV):
        for c in range(tm // TKV):
            vt_ref[g, c, :HEAD_DIM, :] = vt[g * HEAD_DIM:(g + 1) * HEAD_DIM, c * TKV:(c + 1) * TKV].astype(BF16)
            vt_ref[g, c, HEAD_DIM:, :] = ones_rows

    qr = proj(OFF_QR, W_QR)
    kr = proj(OFF_KR, W_KR)
    for b in range(H_RET // 2):
        qo = _rope(qr[:, b * LANES:(b + 1) * LANES], cos, sa, sb)
        ko = _rope(kr[:, b * LANES:(b + 1) * LANES], cos, sa, sb) * (RET_DK ** -0.5)
        qr_ref[2 * b] = jnp.where(low, qo, 0.0).astype(BF16)
        qr_ref[2 * b + 1] = jnp.where(low, 0.0, qo).astype(BF16)
        kr_ref[2 * b] = jnp.where(low, ko, 0.0).astype(BF16)
        kr_ref[2 * b + 1] = jnp.where(low, 0.0, ko).astype(BF16)

    vr_ref[...] = proj(OFF_VR, W_VR).astype(BF16)
    gr = proj(OFF_GR, W_GR)
    gr_ref[...] = (gr * jax.nn.sigmoid(gr)).astype(BF16)
    for c in range(W_GATE // 512):
        z = proj(OFF_GATE + c * 512, 512) + bg_ref[:, c * 512:(c + 1) * 512]
        gate_ref[:, c * 512:(c + 1) * 512] = jax.nn.sigmoid(z).astype(BF16)


def _in_proj(x2, seq_len, norm_g, w_in_p, qg, kg, b_gate, tables, gmat):
    n = x2.shape[0]
    tm = TM_PROJ
    nt = seq_len // tm
    cos, sa, sb = tables
    tab_spec = pl.BlockSpec((tm, LANES), lambda i: (i % nt, 0))
    row = lambda width: pl.BlockSpec((tm, width), lambda i: (i, 0))
    heads = lambda nh: pl.BlockSpec((nh, tm, LANES), lambda i: (0, i, 0))
    n_b = n // seq_len
    return pl.pallas_call(
        _in_proj_kernel,
        grid=(n // tm,),
        in_specs=[row(D_MODEL), _const_spec((1, D_MODEL)), _const_spec((D_MODEL, IN_COLS)),
                  _const_spec((1, LANES)), _const_spec((1, LANES)), _const_spec((1, W_GATE)),
                  tab_spec, tab_spec, tab_spec, _const_spec((2 * LANES, LANES))],
        out_specs=[pl.BlockSpec((N_KV, GROUP, LANES, tm), lambda i: (0, 0, 0, i)),
                   heads(N_KV),
                   pl.BlockSpec((None, N_KV, tm // TKV, VT_ROWS, TKV), lambda i: (i // nt, 0, i % nt, 0, 0)),
                   heads(H_RET), heads(H_RET), row(W_VR), row(W_GR), row(W_GATE)],
        out_shape=[jax.ShapeDtypeStruct((N_KV, GROUP, LANES, n), BF16),
                   jax.ShapeDtypeStruct((N_KV, n, LANES), BF16),
                   jax.ShapeDtypeStruct((n_b, N_KV, seq_len // TKV, VT_ROWS, TKV), BF16),
                   jax.ShapeDtypeStruct((H_RET, n, LANES), BF16),
                   jax.ShapeDtypeStruct((H_RET, n, LANES), BF16),
                   jax.ShapeDtypeStruct((n, W_VR), BF16),
                   jax.ShapeDtypeStruct((n, W_GR), BF16),
                   jax.ShapeDtypeStruct((n, W_GATE), BF16)],
        compiler_params=_cparams(1),
        name="in_proj",
    )(x2, norm_g, w_in_p, qg, kg, b_gate, cos, sa, sb, gmat)


def _attn_kernel(q_ref, k_ref, vt_ref, o_ref, qt_scr, kmax_scr, s_scr, p_scr, acc_scr, m_scr, *, n_kv):
    width = GROUP * TQ

    @pl.when(pl.program_id(2) == 0)
    def _():
        def kbody(j, mx):
            kb = k_ref[pl.ds(pl.multiple_of(j * TKV, TKV), TKV), :].astype(F32)
            n2 = jnp.sum(kb * kb, axis=1, keepdims=True)
            return jnp.maximum(mx, jnp.max(n2, axis=0, keepdims=True))
        kmax_scr[...] = jnp.broadcast_to(lax.fori_loop(0, n_kv, kbody, jnp.zeros((1, 1), F32)), (1, LANES))

    qt = jnp.concatenate([q_ref[hh] for hh in range(GROUP)], axis=1).astype(F32)
    qn2 = jnp.sum(qt * qt, axis=0, keepdims=True)
    bound = jnp.sqrt(qn2 * jnp.tile(kmax_scr[...], (1, width // LANES)))
    rows = lax.broadcasted_iota(jnp.int32, qt.shape, 0)
    qt_scr[...] = jnp.where(rows == HEAD_DIM, -bound, qt).astype(BF16)
    safe = jnp.max(bound) <= SAFE_SHIFT

    def scores(j):
        off = pl.multiple_of(j * TKV, TKV)
        return jnp.dot(k_ref[pl.ds(off, TKV), :], qt_scr[...], preferred_element_type=F32)

    @pl.when(safe)
    def _():
        acc_scr[...] = jnp.zeros_like(acc_scr)

        def body(j, carry):
            p = jnp.exp2(scores(j)).astype(BF16)
            acc_scr[...] += jnp.dot(vt_ref[j], p, preferred_element_type=F32)
            return carry

        lax.fori_loop(0, n_kv, body, 0, unroll=min(n_kv, ATTN_UNROLL))

    @pl.when(jnp.logical_not(safe))
    def _():
        s_scr[...] = scores(0)
        p_scr[...] = jnp.zeros_like(p_scr)
        acc_scr[...] = jnp.zeros_like(acc_scr)
        m_scr[...] = jnp.full_like(m_scr, -1e30)

        def body(j, alpha_prev):
            s_next = scores(jnp.minimum(j + 1, n_kv - 1))
            s = s_scr[...]
            m_old = m_scr[...]
            m_new = jnp.maximum(m_old, jnp.max(s, axis=0, keepdims=True))
            p = jnp.exp2(s - m_new).astype(BF16)
            alpha = jnp.exp2(m_old - m_new)
            pv = jnp.dot(vt_ref[jnp.maximum(j - 1, 0)], p_scr[...], preferred_element_type=F32)
            acc_scr[...] = acc_scr[...] * alpha_prev + pv
            p_scr[...] = p
            s_scr[...] = s_next
            m_scr[...] = m_new
            return alpha

        alpha_last = lax.fori_loop(0, n_kv, body, jnp.ones((1, width), F32))
        acc_scr[...] = acc_scr[...] * alpha_last + jnp.dot(vt_ref[n_kv - 1], p_scr[...],
                                                           preferred_element_type=F32)

    acc = acc_scr[...]
    o = acc[:HEAD_DIM] / acc[HEAD_DIM:HEAD_DIM + 1]
    z = jnp.concatenate([o[:, hh * TQ:(hh + 1) * TQ] for hh in range(GROUP)], axis=0)
    o_ref[...] = z.T.astype(o_ref.dtype)


def _attention(q, k, vt, seq_len):
    n = k.shape[1]
    n_b = n // seq_len
    nq = seq_len // TQ
    n_kv = seq_len // TKV
    width = GROUP * TQ
    return pl.pallas_call(
        functools.partial(_attn_kernel, n_kv=n_kv),
        grid=(n_b, N_KV, nq),
        in_specs=[pl.BlockSpec((None, GROUP, LANES, TQ), lambda b, g, i: (g, 0, 0, b * nq + i)),
                  pl.BlockSpec((None, seq_len, LANES), lambda b, g, i: (g, b, 0)),
                  pl.BlockSpec((None, None, n_kv, VT_ROWS, TKV), lambda b, g, i: (b, g, 0, 0, 0))],
        out_specs=pl.BlockSpec((TQ, GROUP * HEAD_DIM), lambda b, g, i: (b * nq + i, g)),
        out_shape=jax.ShapeDtypeStruct((n, W_QA), BF16),
        scratch_shapes=[pltpu.VMEM((LANES, width), BF16), pltpu.VMEM((1, LANES), F32),
                        pltpu.VMEM((TKV, width), F32), pltpu.VMEM((TKV, width), BF16),
                        pltpu.VMEM((VT_ROWS, width), F32), pltpu.VMEM((1, width), F32)],
        compiler_params=_cparams(3),
        name="attn",
    )(q, k, vt)


def _decay_tables(dec_ref, h):
    lg = jax.nn.log_sigmoid(dec_ref[h])
    return jnp.broadcast_to(lg[0:1, :], (RET_CHUNK, LANES))


def _ret_fwd_kernel(q_ref, k_ref, v_ref, decf_ref, decb_ref, y_ref, r_scr):
    ts = v_ref.shape[0]
    c_len = RET_CHUNK

    @pl.when(pl.program_id(1) == 0)
    def _():
        r_scr[...] = jnp.zeros_like(r_scr)

    ri = lax.broadcasted_iota(jnp.int32, (c_len, LANES), 0).astype(F32)
    ci = lax.broadcasted_iota(jnp.int32, (c_len, LANES), 1).astype(F32)
    diff = ri - ci
    dn_t = (((1,), (1,)), ((), ()))
    dn_kv = (((0,), (0,)), ((), ()))
    for h in range(H_RET):
        lf = _decay_tables(decf_ref, h)
        lb = _decay_tables(decb_ref, h)
        dmat = jnp.where(diff >= 0, jnp.exp(lf * jnp.maximum(diff, 0.0)), jnp.exp(lb * jnp.maximum(-diff, 0.0)))
        qdec = jnp.exp(lf * (ri + 1.0))
        kdec = jnp.exp(lf * (c_len - 1.0 - ri))
        cdec = jnp.exp(lf * float(c_len))
        for c in range(ts // c_len):
            rows = slice(c * c_len, (c + 1) * c_len)
            qh = q_ref[h, rows, :]
            kh = k_ref[h, rows, :]
            vh = v_ref[rows, h * RET_DV:(h + 1) * RET_DV]
            s = lax.dot_general(qh, kh, dn_t, preferred_element_type=F32)
            inner = jnp.dot((s * dmat).astype(BF16), vh, preferred_element_type=F32)
            r = r_scr[h]
            cross = jnp.dot(qh, r.astype(BF16), preferred_element_type=F32) * qdec
            y_ref[rows, h * RET_DV:(h + 1) * RET_DV] = inner + cross
            vd = (vh.astype(F32) * kdec).astype(BF16)
            r_scr[h] = r * cdec + lax.dot_general(kh, vd, dn_kv, preferred_element_type=F32)


def _ret_bwd_kernel(q_ref, k_ref, v_ref, y_ref, gr_ref, decb_ref, o_ref, r_scr):
    ts = v_ref.shape[0]
    c_len = RET_CHUNK

    @pl.when(pl.program_id(1) == 0)
    def _():
        r_scr[...] = jnp.zeros_like(r_scr)

    ri = lax.broadcasted_iota(jnp.int32, (c_len, LANES), 0).astype(F32)
    dn_kv = (((0,), (0,)), ((), ()))
    for h in range(H_RET):
        lb = _decay_tables(decb_ref, h)
        qdec = jnp.exp(lb * (float(c_len) - ri))
        kdec = jnp.exp(lb * ri)
        cdec = jnp.exp(lb * float(c_len))
        for c in reversed(range(ts // c_len)):
            rows = slice(c * c_len, (c + 1) * c_len)
            cols = slice(h * RET_DV, (h + 1) * RET_DV)
            qh = q_ref[h, rows, :]
            kh = k_ref[h, rows, :]
            vh = v_ref[rows, cols]
            r = r_scr[h]
            y = y_ref[rows, cols] + jnp.dot(qh, r.astype(BF16), preferred_element_type=F32) * qdec
            yn = y * lax.rsqrt(jnp.mean(y * y, axis=-1, keepdims=True) + EPS)
            o_ref[rows, cols] = (gr_ref[rows, cols].astype(F32) * yn).astype(o_ref.dtype)
            vd = (vh.astype(F32) * kdec).astype(BF16)
            r_scr[h] = r * cdec + lax.dot_general(kh, vd, dn_kv, preferred_element_type=F32)


def _retention(qr, kr, vr, gr, decf, decb, seq_len):
    n = vr.shape[0]
    n_b = n // seq_len
    ts = TS_RET
    nt = seq_len // ts
    fwd = lambda b, j: b * nt + j
    bwd = lambda b, j: b * nt + (nt - 1 - j)
    heads = lambda f: pl.BlockSpec((H_RET, ts, LANES), lambda b, j: (0, f(b, j), 0))
    row = lambda f: pl.BlockSpec((ts, W_VR), lambda b, j: (f(b, j), 0))
    dec_spec = _const_spec((H_RET, 8, LANES))
    state = pltpu.VMEM((H_RET, LANES, RET_DV), F32)
    y1 = pl.pallas_call(
        _ret_fwd_kernel,
        grid=(n_b, nt),
        in_specs=[heads(fwd), heads(fwd), row(fwd), dec_spec, dec_spec],
        out_specs=row(fwd),
        out_shape=jax.ShapeDtypeStruct((n, W_VR), F32),
        scratch_shapes=[state],
        compiler_params=_cparams(2),
        name="ret_fwd",
    )(qr, kr, vr, decf, decb)
    return pl.pallas_call(
        _ret_bwd_kernel,
        grid=(n_b, nt),
        in_specs=[heads(bwd), heads(bwd), row(bwd), row(bwd), row(bwd), dec_spec],
        out_specs=row(bwd),
        out_shape=jax.ShapeDtypeStruct((n, W_VR), BF16),
        scratch_shapes=[state],
        compiler_params=_cparams(2),
        name="ret_bwd",
    )(qr, kr, vr, y1, gr, decb)


def _merge_kernel(x_ref, a_ref, r_ref, gate_ref, wa_ref, wr_ref, wo_ref, o_ref):
    ad = jnp.dot(a_ref[...], wa_ref[...], preferred_element_type=F32)
    rd = jnp.dot(r_ref[...], wr_ref[...], preferred_element_type=F32)
    g_a = gate_ref[:, :D_MODEL].astype(F32)
    g_r = gate_ref[:, D_MODEL:].astype(F32)
    m = (g_a * ad + g_r * rd).astype(BF16)
    o_ref[...] = x_ref[...] + jnp.dot(m, wo_ref[...], preferred_element_type=F32)


def _merge(x2, attn, ret, gates, wa, wr, wo):
    n = x2.shape[0]
    tm = TM_PROJ
    row = lambda width: pl.BlockSpec((tm, width), lambda i: (i, 0))
    return pl.pallas_call(
        _merge_kernel,
        grid=(n // tm,),
        in_specs=[row(D_MODEL), row(W_QA), row(W_VR), row(W_GATE),
                  _const_spec((W_QA, D_MODEL)), _const_spec((W_VR, D_MODEL)), _const_spec((D_MODEL, D_MODEL))],
        out_specs=row(D_MODEL),
        out_shape=jax.ShapeDtypeStruct((n, D_MODEL), F32),
        compiler_params=_cparams(1),
        name="merge",
    )(x2, attn, ret, gates, wa, wr, wo)


def _ffn_kernel(xp_ref, x_ref, xn_ref, g2_ref, wup_ref, cw_ref, cb_ref, wdn_ref, o_ref, act_scr, *, seq_len):
    tm = x_ref.shape[0]
    rows = tm + 2 * HALO
    i = pl.program_id(0)
    first = (i * tm) % seq_len == 0
    last = ((i + 1) * tm) % seq_len == 0

    def norm(x):
        return x * lax.rsqrt(jnp.mean(x * x, axis=-1, keepdims=True) + EPS) * g2_ref[...]

    hp = jnp.where(first, 0.0, norm(xp_ref[...]))
    hn = jnp.where(last, 0.0, norm(xn_ref[...]))
    x = x_ref[...]
    h = jnp.concatenate([hp, norm(x), hn], axis=0).astype(BF16)

    def conv(c0):
        u = jnp.dot(h, wup_ref[:, c0:c0 + FC], preferred_element_type=F32)
        w = cw_ref[:, c0:c0 + FC]
        down = pltpu.roll(u, 1, 0)[HALO:HALO + tm]
        up = pltpu.roll(u, rows - 1, 0)[HALO:HALO + tm]
        return down * w[0:1] + u[HALO:HALO + tm] * w[1:2] + up * w[2:3] + cb_ref[:, c0:c0 + FC]

    for f in range(D_FF // FC):
        c0 = f * FC
        val = conv(c0)
        gt = conv(D_FF + c0)
        gelu = 0.5 * gt * (1.0 + lax.erf(gt * (1.0 / math.sqrt(2.0))))
        act_scr[:, c0:c0 + FC] = (gelu * val).astype(BF16)
    o_ref[...] = x + jnp.dot(act_scr[...], wdn_ref[...], preferred_element_type=F32)


def _ffn(x2, seq_len, norm_g, w_up, conv_w, conv_b, w_down):
    n = x2.shape[0]
    tm = TM_PROJ
    hb = tm // HALO
    n_hb = n // HALO
    return pl.pallas_call(
        functools.partial(_ffn_kernel, seq_len=seq_len),
        grid=(n // tm,),
        in_specs=[pl.BlockSpec((HALO, D_MODEL), lambda i: (jnp.maximum(i * hb - 1, 0), 0)),
                  pl.BlockSpec((tm, D_MODEL), lambda i: (i, 0)),
                  pl.BlockSpec((HALO, D_MODEL), lambda i: (jnp.minimum((i + 1) * hb, n_hb - 1), 0)),
                  _const_spec((1, D_MODEL)), _const_spec((D_MODEL, 2 * D_FF)),
                  _const_spec((3, 2 * D_FF)), _const_spec((1, 2 * D_FF)), _const_spec((D_FF, D_MODEL))],
        out_specs=pl.BlockSpec((tm, D_MODEL), lambda i: (i, 0)),
        out_shape=jax.ShapeDtypeStruct((n, D_MODEL), F32),
        scratch_shapes=[pltpu.VMEM((tm, D_FF), BF16)],
        compiler_params=_cparams(1),
        name="ffn",
    )(x2, x2, x2, norm_g, w_up, conv_w, conv_b, w_down)


def _layer(x2, seq_len, tables, gmat, p):
    q, k, vt, qr, kr, vr, gr, gates = _in_proj(x2, seq_len, p["norm1_g"], p["w_in"], p["qg"], p["kg"],
                                               p["b_gate"], tables, gmat)
    attn = _attention(q, k, vt, seq_len)
    ret = _retention(qr, kr, vr, gr, p["decf"], p["decb"], seq_len)
    x2 = _merge(x2, attn, ret, gates, p["w_attn_o"], p["w_ret_o"], p["w_out"])
    return _ffn(x2, seq_len, p["norm2_g"], p["w_up"], p["conv_w"], p["conv_b"], p["w_down"])


def _layer_params(l, norm1_g, w_in, q_norm_g, k_norm_g, ret_decay_fwd, ret_decay_bwd, w_attn_o, w_ret_o,
                  b_gate, w_out, norm2_g, w_up, conv_w, conv_b, w_down):
    head_order = [hh for b in range(GROUP) for hh in (b, GROUP + b)]
    q_cols = jnp.asarray([hh * HEAD_DIM + d for hh in head_order for d in range(HEAD_DIM)], jnp.int32)
    w = w_in[l]
    w_p = jnp.concatenate([jnp.take(w[:, :W_QA], q_cols, axis=1), w[:, W_QA:]], axis=1).astype(BF16)
    two = lambda g: jnp.concatenate([g, g]).reshape(1, LANES)
    bcast = lambda d: jnp.broadcast_to(d.astype(F32)[:, None, None], (H_RET, 8, LANES))
    return dict(
        norm1_g=norm1_g[l].reshape(1, D_MODEL), w_in=w_p,
        qg=two(q_norm_g[l]) * (LOG2E / math.sqrt(HEAD_DIM)), kg=two(k_norm_g[l]),
        decf=bcast(ret_decay_fwd[l]), decb=bcast(ret_decay_bwd[l]),
        w_attn_o=w_attn_o[l].astype(BF16), w_ret_o=w_ret_o[l].astype(BF16),
        b_gate=b_gate[l].reshape(1, W_GATE), w_out=w_out[l].astype(BF16),
        norm2_g=norm2_g[l].reshape(1, D_MODEL), w_up=w_up[l].astype(BF16),
        conv_w=conv_w[l], conv_b=conv_b[l].reshape(1, 2 * D_FF), w_down=w_down[l].astype(BF16))


def kernel(x_prompt, x_sample, norm1_g, w_in, q_norm_g, k_norm_g, ret_decay_fwd, ret_decay_bwd, w_attn_o, w_ret_o, b_gate, w_out, norm2_g, w_up, conv_w, conv_b, w_down):
    weights = (norm1_g, w_in, q_norm_g, k_norm_g, ret_decay_fwd, ret_decay_bwd, w_attn_o, w_ret_o,
               b_gate, w_out, norm2_g, w_up, conv_w, conv_b, w_down)
    depth = w_in.shape[0]
    lane = jnp.arange(2 * LANES) % LANES
    gmat = ((lane[:, None] // HEAD_DIM) == (jnp.arange(LANES)[None, :] // HEAD_DIM)).astype(BF16) / HEAD_DIM
    outs = []
    for x in (x_prompt, x_sample):
        n_b, seq_len, _ = x.shape
        tables = _rope_tables(seq_len)
        x2 = x.reshape(n_b * seq_len, D_MODEL)
        for l in range(depth):
            x2 = _layer(x2, seq_len, tables, gmat, _layer_params(l, *weights))
        outs.append(x2.reshape(n_b, seq_len, D_MODEL))
    return tuple(outs)
```

```python
import functools
import math

import jax
import jax.numpy as jnp
from jax import lax
from jax.experimental import pallas as pl
from jax.experimental.pallas import tpu as pltpu

F32 = jnp.float32
BF16 = jnp.bfloat16

D_MODEL = 1024
GRID_W = 64
HEAD_DIM = 64
N_HEADS = 8
N_KV = 2
GROUP = N_HEADS // N_KV
ROPE_THETA = 10000.0
N_FREQ = HEAD_DIM // 4
H_RET = 4
RET_DK = 64
RET_DV = 128
RET_CHUNK = 128
D_FF = 2816
EPS = 1e-6

W_QA = N_HEADS * HEAD_DIM
W_KA = N_KV * HEAD_DIM
W_VA = N_KV * HEAD_DIM
W_QR = H_RET * RET_DK
W_KR = H_RET * RET_DK
W_VR = H_RET * RET_DV
W_GR = H_RET * RET_DV
W_GATE = 2 * D_MODEL
OFF_KA = W_QA
OFF_VA = OFF_KA + W_KA
OFF_QR = OFF_VA + W_VA
OFF_KR = OFF_QR + W_QR
OFF_VR = OFF_KR + W_KR
OFF_GR = OFF_VR + W_VR
OFF_GATE = OFF_GR + W_GR
IN_COLS = OFF_GATE + W_GATE

LANES = 128
SUBLANES = 8
BF16_ROWS = 16
LOG2E = 1.4426950408889634

TM_PROJ = 512
TQ = 256
TKV = 512
ATTN_UNROLL = 4
BOUND_MARGIN = 1.01
SAFE_SHIFT = 60.0
VT_ROWS = HEAD_DIM + BF16_ROWS
TS_RET = 512
FC = 256
HALO = 8
VMEM_LIMIT = 56 * 1024 * 1024


def _cparams(n_axes):
    return pltpu.CompilerParams(dimension_semantics=("arbitrary",) * n_axes, vmem_limit_bytes=VMEM_LIMIT)


def _const_spec(shape):
    nd = len(shape)
    return pl.BlockSpec(shape, lambda *_: (0,) * nd, pipeline_mode=pl.Buffered(1))


def _rope_tables(seq_len):
    t = jnp.arange(seq_len, dtype=F32)
    row = jnp.floor(t / GRID_W)
    col = t - row * GRID_W
    freqs = 1.0 / (ROPE_THETA ** (jnp.arange(N_FREQ, dtype=F32) / N_FREQ))
    ang = jnp.stack([row[:, None] * freqs, col[:, None] * freqs], axis=1)
    cos = jnp.cos(ang)
    sin = jnp.sin(ang)
    zero = jnp.zeros_like(sin)
    cos_h = jnp.stack([cos, cos], axis=2).reshape(seq_len, HEAD_DIM)
    sa_h = jnp.stack([-sin, zero], axis=2).reshape(seq_len, HEAD_DIM)
    sb_h = jnp.stack([zero, sin], axis=2).reshape(seq_len, HEAD_DIM)
    tile2 = lambda a: jnp.concatenate([a, a], axis=1)
    return tile2(cos_h), tile2(sa_h), tile2(sb_h)


def _rope(x, cos, sa, sb):
    return x * cos + pltpu.roll(x, LANES - N_FREQ, 1) * sa + pltpu.roll(x, N_FREQ, 1) * sb


def _in_proj_kernel(x_ref, g1_ref, w_ref, qg_ref, kg_ref, shift_ref, bg_ref, cos_ref, sa_ref, sb_ref, gm_ref,
                    q_ref, k_ref, vt_ref, qr_ref, kr_ref, vr_ref, gr_ref, gate_ref):
    tm = x_ref.shape[0]
    x = x_ref[...]
    h = (x * lax.rsqrt(jnp.mean(x * x, axis=-1, keepdims=True) + EPS) * g1_ref[...]).astype(BF16)
    cos = cos_ref[...]
    sa = sa_ref[...]
    sb = sb_ref[...]
    lane = lax.broadcasted_iota(jnp.int32, (tm, LANES), 1)
    low = lane < HEAD_DIM

    def proj(c0, width):
        return jnp.dot(h, w_ref[:, c0:c0 + width], preferred_element_type=F32)

    def head_rms(blk):
        ss = blk * blk
        hi = ss.astype(BF16)
        lo = (ss - hi.astype(F32)).astype(BF16)
        ms = jnp.dot(jnp.concatenate([hi, lo], axis=1), gm_ref[...], preferred_element_type=F32)
        return lax.rsqrt(ms + EPS)

    qa = proj(0, W_QA)
    shift_rows = jnp.where(lax.broadcasted_iota(jnp.int32, (HEAD_DIM, LANES), 0) == 0, -shift_ref[...], 0.0)
    shift_rows = jnp.tile(shift_rows, (1, GROUP * TQ // LANES)).astype(BF16)
    for g in range(N_KV):
        for t in range(tm // TQ):
            q_ref[g, t, HEAD_DIM:, :] = shift_rows
    for b in range(GROUP):
        blk = qa[:, b * LANES:(b + 1) * LANES]
        out_t = _rope(blk * head_rms(blk) * qg_ref[...], cos, sa, sb).T
        for g in range(N_KV):
            for t in range(tm // TQ):
                q_ref[g, t, :HEAD_DIM, b * TQ:(b + 1) * TQ] = (
                    out_t[g * HEAD_DIM:(g + 1) * HEAD_DIM, t * TQ:(t + 1) * TQ].astype(BF16))

    kv = proj(OFF_KA, W_KA + W_VA)
    kblk = kv[:, :LANES]
    kout = _rope(kblk * head_rms(kblk) * kg_ref[...], cos, sa, sb)
    one_lane = jnp.where(lane == HEAD_DIM, 1.0, 0.0)
    k_ref[0] = jnp.where(low, kout, one_lane).astype(BF16)
    k_ref[1] = jnp.where(low, pltpu.roll(kout, HEAD_DIM, 1), one_lane).astype(BF16)
    vt = kv[:, LANES:].T
    ones_rows = (lax.broadcasted_iota(jnp.int32, (BF16_ROWS, TKV), 0) == 0).astype(BF16)
    for g in range(N_KV):
        for c in range(tm // TKV):
            vt_ref[g, c, :HEAD_DIM, :] = vt[g * HEAD_DIM:(g + 1) * HEAD_DIM, c * TKV:(c + 1) * TKV].astype(BF16)
            vt_ref[g, c, HEAD_DIM:, :] = ones_rows

    qr = proj(OFF_QR, W_QR)
    kr = proj(OFF_KR, W_KR)
    for b in range(H_RET // 2):
        qo = _rope(qr[:, b * LANES:(b + 1) * LANES], cos, sa, sb)
        ko = _rope(kr[:, b * LANES:(b + 1) * LANES], cos, sa, sb) * (RET_DK ** -0.5)
        qr_ref[2 * b] = jnp.where(low, qo, 0.0).astype(BF16)
        qr_ref[2 * b + 1] = jnp.where(low, 0.0, qo).astype(BF16)
        kr_ref[2 * b] = jnp.where(low, ko, 0.0).astype(BF16)
        kr_ref[2 * b + 1] = jnp.where(low, 0.0, ko).astype(BF16)

    vr_ref[...] = proj(OFF_VR, W_VR).astype(BF16)
    gr = proj(OFF_GR, W_GR)
    gr_ref[...] = (gr * jax.nn.sigmoid(gr)).astype(BF16)
    for c in range(W_GATE // 512):
        z = proj(OFF_GATE + c * 512, 512) + bg_ref[:, c * 512:(c + 1) * 512]
        gate_ref[:, c * 512:(c + 1) * 512] = jax.nn.sigmoid(z).astype(BF16)


def _in_proj(x2, seq_len, norm_g, w_in_p, qg, kg, shift, b_gate, tables, gmat):
    n = x2.shape[0]
    tm = TM_PROJ
    nt = seq_len // tm
    cos, sa, sb = tables
    tab_spec = pl.BlockSpec((tm, LANES), lambda i: (i % nt, 0))
    row = lambda width: pl.BlockSpec((tm, width), lambda i: (i, 0))
    heads = lambda nh: pl.BlockSpec((nh, tm, LANES), lambda i: (0, i, 0))
    n_b = n // seq_len
    return pl.pallas_call(
        _in_proj_kernel,
        grid=(n // tm,),
        in_specs=[row(D_MODEL), _const_spec((1, D_MODEL)), _const_spec((D_MODEL, IN_COLS)),
                  _const_spec((1, LANES)), _const_spec((1, LANES)), _const_spec((1, LANES)),
                  _const_spec((1, W_GATE)), tab_spec, tab_spec, tab_spec, _const_spec((2 * LANES, LANES))],
        out_specs=[pl.BlockSpec((N_KV, tm // TQ, LANES, GROUP * TQ), lambda i: (0, i, 0, 0)),
                   heads(N_KV),
                   pl.BlockSpec((None, N_KV, tm // TKV, VT_ROWS, TKV), lambda i: (i // nt, 0, i % nt, 0, 0)),
                   heads(H_RET), heads(H_RET), row(W_VR), row(W_GR), row(W_GATE)],
        out_shape=[jax.ShapeDtypeStruct((N_KV, n // TQ, LANES, GROUP * TQ), BF16),
                   jax.ShapeDtypeStruct((N_KV, n, LANES), BF16),
                   jax.ShapeDtypeStruct((n_b, N_KV, seq_len // TKV, VT_ROWS, TKV), BF16),
                   jax.ShapeDtypeStruct((H_RET, n, LANES), BF16),
                   jax.ShapeDtypeStruct((H_RET, n, LANES), BF16),
                   jax.ShapeDtypeStruct((n, W_VR), BF16),
                   jax.ShapeDtypeStruct((n, W_GR), BF16),
                   jax.ShapeDtypeStruct((n, W_GATE), BF16)],
        compiler_params=_cparams(1),
        name="in_proj",
    )(x2, norm_g, w_in_p, qg, kg, shift, b_gate, cos, sa, sb, gmat)


def _attn_kernel(shift_ref, q_ref, k_ref, vt_ref, o_ref, s_scr, p_scr, acc_scr, m_scr, *, n_kv):
    width = GROUP * TQ
    safe = shift_ref[0] <= SAFE_SHIFT

    def scores(j):
        off = pl.multiple_of(j * TKV, TKV)
        return jnp.dot(k_ref[pl.ds(off, TKV), :], q_ref[...], preferred_element_type=F32)

    @pl.when(safe)
    def _():
        acc_scr[...] = jnp.zeros_like(acc_scr)

        def body(j, carry):
            p = jnp.exp2(scores(j)).astype(BF16)
            acc_scr[...] += jnp.dot(vt_ref[j], p, preferred_element_type=F32)
            return carry

        lax.fori_loop(0, n_kv, body, 0, unroll=min(n_kv, ATTN_UNROLL))

    @pl.when(jnp.logical_not(safe))
    def _():
        s_scr[...] = scores(0)
        p_scr[...] = jnp.zeros_like(p_scr)
        acc_scr[...] = jnp.zeros_like(acc_scr)
        m_scr[...] = jnp.full_like(m_scr, -1e30)

        def body(j, alpha_prev):
            s_next = scores(jnp.minimum(j + 1, n_kv - 1))
            s = s_scr[...]
            m_old = m_scr[...]
            m_new = jnp.maximum(m_old, jnp.max(s, axis=0, keepdims=True))
            p = jnp.exp2(s - m_new).astype(BF16)
            alpha = jnp.exp2(m_old - m_new)
            pv = jnp.dot(vt_ref[jnp.maximum(j - 1, 0)], p_scr[...], preferred_element_type=F32)
            acc_scr[...] = acc_scr[...] * alpha_prev + pv
            p_scr[...] = p
            s_scr[...] = s_next
            m_scr[...] = m_new
            return alpha

        alpha_last = lax.fori_loop(0, n_kv, body, jnp.ones((1, width), F32))
        acc_scr[...] = acc_scr[...] * alpha_last + jnp.dot(vt_ref[n_kv - 1], p_scr[...],
                                                           preferred_element_type=F32)

    acc = acc_scr[...]
    o = acc[:HEAD_DIM] / acc[HEAD_DIM:HEAD_DIM + 1]
    z = jnp.concatenate([o[:, hh * TQ:(hh + 1) * TQ] for hh in range(GROUP)], axis=0)
    o_ref[...] = z.T.astype(o_ref.dtype)


def _attention(shift, q, k, vt, seq_len):
    n = k.shape[1]
    n_b = n // seq_len
    nq = seq_len // TQ
    n_kv = seq_len // TKV
    width = GROUP * TQ
    return pl.pallas_call(
        functools.partial(_attn_kernel, n_kv=n_kv),
        grid=(n_b, N_KV, nq),
        in_specs=[pl.BlockSpec(memory_space=pltpu.SMEM),
                  pl.BlockSpec((None, None, LANES, width), lambda b, g, i: (g, b * nq + i, 0, 0)),
                  pl.BlockSpec((None, seq_len, LANES), lambda b, g, i: (g, b, 0)),
                  pl.BlockSpec((None, None, n_kv, VT_ROWS, TKV), lambda b, g, i: (b, g, 0, 0, 0))],
        out_specs=pl.BlockSpec((TQ, GROUP * HEAD_DIM), lambda b, g, i: (b * nq + i, g)),
        out_shape=jax.ShapeDtypeStruct((n, W_QA), BF16),
        scratch_shapes=[pltpu.VMEM((TKV, width), F32), pltpu.VMEM((TKV, width), BF16),
                        pltpu.VMEM((VT_ROWS, width), F32), pltpu.VMEM((1, width), F32)],
        compiler_params=_cparams(3),
        name="attn",
    )(shift, q, k, vt)


def _decay_tables(dec_ref, h):
    lg = jax.nn.log_sigmoid(dec_ref[h])
    return jnp.broadcast_to(lg[0:1, :], (RET_CHUNK, LANES))


def _ret_fwd_kernel(q_ref, k_ref, v_ref, decf_ref, decb_ref, y_ref, r_scr):
    ts = v_ref.shape[0]
    c_len = RET_CHUNK
    n_c = ts // c_len

    @pl.when(pl.program_id(1) == 0)
    def _():
        r_scr[...] = jnp.zeros_like(r_scr)

    ri = lax.broadcasted_iota(jnp.int32, (c_len, LANES), 0).astype(F32)
    ci = lax.broadcasted_iota(jnp.int32, (c_len, LANES), 1).astype(F32)
    diff = ri - ci
    dn_t = (((1,), (1,)), ((), ()))
    dn_kv = (((0,), (0,)), ((), ()))
    blk = lambda h, c: (slice(c * c_len, (c + 1) * c_len), slice(h * RET_DV, (h + 1) * RET_DV))
    kv = {}
    for h in range(H_RET):
        lf = _decay_tables(decf_ref, h)
        lb = _decay_tables(decb_ref, h)
        dmat = jnp.where(diff >= 0, jnp.exp(lf * jnp.maximum(diff, 0.0)), jnp.exp(lb * jnp.maximum(-diff, 0.0)))
        kdec = jnp.exp(lf * (c_len - 1.0 - ri))
        for c in range(n_c):
            rows, cols = blk(h, c)
            qh = q_ref[h, rows, :]
            kh = k_ref[h, rows, :]
            vh = v_ref[rows, cols]
            s = lax.dot_general(qh, kh, dn_t, preferred_element_type=F32)
            y_ref[rows, cols] = jnp.dot((s * dmat).astype(BF16), vh, preferred_element_type=F32)
            vd = (vh.astype(F32) * kdec).astype(BF16)
            kv[h, c] = lax.dot_general(kh, vd, dn_kv, preferred_element_type=F32)
    for h in range(H_RET):
        lf = _decay_tables(decf_ref, h)
        qdec = jnp.exp(lf * (ri + 1.0))
        cdec = jnp.exp(lf * float(c_len))
        r = r_scr[h]
        for c in range(n_c):
            rows, cols = blk(h, c)
            y_ref[rows, cols] += jnp.dot(q_ref[h, rows, :], r.astype(BF16), preferred_element_type=F32) * qdec
            r = r * cdec + kv[h, c]
        r_scr[h] = r


def _ret_bwd_kernel(q_ref, k_ref, v_ref, y_ref, gr_ref, decb_ref, o_ref, r_scr):
    ts = v_ref.shape[0]
    c_len = RET_CHUNK
    n_c = ts // c_len

    @pl.when(pl.program_id(1) == 0)
    def _():
        r_scr[...] = jnp.zeros_like(r_scr)

    ri = lax.broadcasted_iota(jnp.int32, (c_len, LANES), 0).astype(F32)
    dn_kv = (((0,), (0,)), ((), ()))
    blk = lambda h, c: (slice(c * c_len, (c + 1) * c_len), slice(h * RET_DV, (h + 1) * RET_DV))
    kv = {}
    for h in range(H_RET):
        kdec = jnp.exp(_decay_tables(decb_ref, h) * ri)
        for c in range(n_c):
            rows, cols = blk(h, c)
            vd = (v_ref[rows, cols].astype(F32) * kdec).astype(BF16)
            kv[h, c] = lax.dot_general(k_ref[h, rows, :], vd, dn_kv, preferred_element_type=F32)
    for h in range(H_RET):
        lb = _decay_tables(decb_ref, h)
        qdec = jnp.exp(lb * (float(c_len) - ri))
        cdec = jnp.exp(lb * float(c_len))
        r = r_scr[h]
        for c in reversed(range(n_c)):
            rows, cols = blk(h, c)
            y = y_ref[rows, cols] + jnp.dot(q_ref[h, rows, :], r.astype(BF16), preferred_element_type=F32) * qdec
            yn = y * lax.rsqrt(jnp.mean(y * y, axis=-1, keepdims=True) + EPS)
            o_ref[rows, cols] = (gr_ref[rows, cols].astype(F32) * yn).astype(o_ref.dtype)
            r = r * cdec + kv[h, c]
        r_scr[h] = r


def _retention(qr, kr, vr, gr, decf, decb, seq_len):
    n = vr.shape[0]
    n_b = n // seq_len
    ts = TS_RET
    nt = seq_len // ts
    fwd = lambda b, j: b * nt + j
    bwd = lambda b, j: b * nt + (nt - 1 - j)
    heads = lambda f: pl.BlockSpec((H_RET, ts, LANES), lambda b, j: (0, f(b, j), 0))
    row = lambda f: pl.BlockSpec((ts, W_VR), lambda b, j: (f(b, j), 0))
    dec_spec = _const_spec((H_RET, 8, LANES))
    state = pltpu.VMEM((H_RET, LANES, RET_DV), F32)
    y1 = pl.pallas_call(
        _ret_fwd_kernel,
        grid=(n_b, nt),
        in_specs=[heads(fwd), heads(fwd), row(fwd), dec_spec, dec_spec],
        out_specs=row(fwd),
        out_shape=jax.ShapeDtypeStruct((n, W_VR), F32),
        scratch_shapes=[state],
        compiler_params=_cparams(2),
        name="ret_fwd",
    )(qr, kr, vr, decf, decb)
    return pl.pallas_call(
        _ret_bwd_kernel,
        grid=(n_b, nt),
        in_specs=[heads(bwd), heads(bwd), row(bwd), row(bwd), row(bwd), dec_spec],
        out_specs=row(bwd),
        out_shape=jax.ShapeDtypeStruct((n, W_VR), BF16),
        scratch_shapes=[state],
        compiler_params=_cparams(2),
        name="ret_bwd",
    )(qr, kr, vr, y1, gr, decb)


def _merge_kernel(x_ref, a_ref, r_ref, gate_ref, wa_ref, wr_ref, wo_ref, o_ref):
    ad = jnp.dot(a_ref[...], wa_ref[...], preferred_element_type=F32)
    rd = jnp.dot(r_ref[...], wr_ref[...], preferred_element_type=F32)
    g_a = gate_ref[:, :D_MODEL].astype(F32)
    g_r = gate_ref[:, D_MODEL:].astype(F32)
    m = (g_a * ad + g_r * rd).astype(BF16)
    o_ref[...] = x_ref[...] + jnp.dot(m, wo_ref[...], preferred_element_type=F32)


def _merge(x2, attn, ret, gates, wa, wr, wo):
    n = x2.shape[0]
    tm = TM_PROJ
    row = lambda width: pl.BlockSpec((tm, width), lambda i: (i, 0))
    return pl.pallas_call(
        _merge_kernel,
        grid=(n // tm,),
        in_specs=[row(D_MODEL), row(W_QA), row(W_VR), row(W_GATE),
                  _const_spec((W_QA, D_MODEL)), _const_spec((W_VR, D_MODEL)), _const_spec((D_MODEL, D_MODEL))],
        out_specs=row(D_MODEL),
        out_shape=jax.ShapeDtypeStruct((n, D_MODEL), F32),
        compiler_params=_cparams(1),
        name="merge",
    )(x2, attn, ret, gates, wa, wr, wo)


def _ffn_kernel(xp_ref, x_ref, xn_ref, g2_ref, wup_ref, cw_ref, cb_ref, wdn_ref, o_ref, act_scr, *, seq_len):
    tm = x_ref.shape[0]
    rows = tm + 2 * HALO
    i = pl.program_id(0)
    first = (i * tm) % seq_len == 0
    last = ((i + 1) * tm) % seq_len == 0

    def norm(x):
        return x * lax.rsqrt(jnp.mean(x * x, axis=-1, keepdims=True) + EPS) * g2_ref[...]

    hp = jnp.where(first, 0.0, norm(xp_ref[...]))
    hn = jnp.where(last, 0.0, norm(xn_ref[...]))
    x = x_ref[...]
    h = jnp.concatenate([hp, norm(x), hn], axis=0).astype(BF16)

    def conv(c0):
        u = jnp.dot(h, wup_ref[:, c0:c0 + FC], preferred_element_type=F32)
        w = cw_ref[:, c0:c0 + FC]
        down = pltpu.roll(u, 1, 0)[HALO:HALO + tm]
        up = pltpu.roll(u, rows - 1, 0)[HALO:HALO + tm]
        return down * w[0:1] + u[HALO:HALO + tm] * w[1:2] + up * w[2:3] + cb_ref[:, c0:c0 + FC]

    for f in range(D_FF // FC):
        c0 = f * FC
        val = conv(c0)
        gt = conv(D_FF + c0)
        gelu = 0.5 * gt * (1.0 + lax.erf(gt * (1.0 / math.sqrt(2.0))))
        act_scr[:, c0:c0 + FC] = (gelu * val).astype(BF16)
    o_ref[...] = x + jnp.dot(act_scr[...], wdn_ref[...], preferred_element_type=F32)


def _ffn(x2, seq_len, norm_g, w_up, conv_w, conv_b, w_down):
    n = x2.shape[0]
    tm = TM_PROJ
    hb = tm // HALO
    n_hb = n // HALO
    return pl.pallas_call(
        functools.partial(_ffn_kernel, seq_len=seq_len),
        grid=(n // tm,),
        in_specs=[pl.BlockSpec((HALO, D_MODEL), lambda i: (jnp.maximum(i * hb - 1, 0), 0)),
                  pl.BlockSpec((tm, D_MODEL), lambda i: (i, 0)),
                  pl.BlockSpec((HALO, D_MODEL), lambda i: (jnp.minimum((i + 1) * hb, n_hb - 1), 0)),
                  _const_spec((1, D_MODEL)), _const_spec((D_MODEL, 2 * D_FF)),
                  _const_spec((3, 2 * D_FF)), _const_spec((1, 2 * D_FF)), _const_spec((D_FF, D_MODEL))],
        out_specs=pl.BlockSpec((tm, D_MODEL), lambda i: (i, 0)),
        out_shape=jax.ShapeDtypeStruct((n, D_MODEL), F32),
        scratch_shapes=[pltpu.VMEM((tm, D_FF), BF16)],
        compiler_params=_cparams(1),
        name="ffn",
    )(x2, x2, x2, norm_g, w_up, conv_w, conv_b, w_down)


def _layer(x2, seq_len, tables, gmat, p):
    q, k, vt, qr, kr, vr, gr, gates = _in_proj(x2, seq_len, p["norm1_g"], p["w_in"], p["qg"], p["kg"],
                                               p["shift_row"], p["b_gate"], tables, gmat)
    attn = _attention(p["shift"], q, k, vt, seq_len)
    ret = _retention(qr, kr, vr, gr, p["decf"], p["decb"], seq_len)
    x2 = _merge(x2, attn, ret, gates, p["w_attn_o"], p["w_ret_o"], p["w_out"])
    return _ffn(x2, seq_len, p["norm2_g"], p["w_up"], p["conv_w"], p["conv_b"], p["w_down"])


def _layer_params(l, norm1_g, w_in, q_norm_g, k_norm_g, ret_decay_fwd, ret_decay_bwd, w_attn_o, w_ret_o,
                  b_gate, w_out, norm2_g, w_up, conv_w, conv_b, w_down):
    head_order = [hh for b in range(GROUP) for hh in (b, GROUP + b)]
    q_cols = jnp.asarray([hh * HEAD_DIM + d for hh in head_order for d in range(HEAD_DIM)], jnp.int32)
    w = w_in[l]
    w_p = jnp.concatenate([jnp.take(w[:, :W_QA], q_cols, axis=1), w[:, W_QA:]], axis=1).astype(BF16)
    two = lambda g: jnp.concatenate([g, g]).reshape(1, LANES)
    bcast = lambda d: jnp.broadcast_to(d.astype(F32)[:, None, None], (H_RET, 8, LANES))
    shift = (BOUND_MARGIN * HEAD_DIM * LOG2E / math.sqrt(HEAD_DIM)) * (
        jnp.max(jnp.abs(q_norm_g[l])) * jnp.max(jnp.abs(k_norm_g[l]))).astype(F32)
    return dict(
        norm1_g=norm1_g[l].reshape(1, D_MODEL), w_in=w_p,
        qg=two(q_norm_g[l]) * (LOG2E / math.sqrt(HEAD_DIM)), kg=two(k_norm_g[l]),
        shift=shift.reshape(1), shift_row=jnp.broadcast_to(shift, (1, LANES)),
        decf=bcast(ret_decay_fwd[l]), decb=bcast(ret_decay_bwd[l]),
        w_attn_o=w_attn_o[l].astype(BF16), w_ret_o=w_ret_o[l].astype(BF16),
        b_gate=b_gate[l].reshape(1, W_GATE), w_out=w_out[l].astype(BF16),
        norm2_g=norm2_g[l].reshape(1, D_MODEL), w_up=w_up[l].astype(BF16),
        conv_w=conv_w[l], conv_b=conv_b[l].reshape(1, 2 * D_FF), w_down=w_down[l].astype(BF16))


def kernel(x_prompt, x_sample, norm1_g, w_in, q_norm_g, k_norm_g, ret_decay_fwd, ret_decay_bwd, w_attn_o, w_ret_o, b_gate, w_out, norm2_g, w_up, conv_w, conv_b, w_down):
    weights = (norm1_g, w_in, q_norm_g, k_norm_g, ret_decay_fwd, ret_decay_bwd, w_attn_o, w_ret_o,
               b_gate, w_out, norm2_g, w_up, conv_w, conv_b, w_down)
    depth = w_in.shape[0]
    lane = jnp.arange(2 * LANES) % LANES
    gmat = ((lane[:, None] // HEAD_DIM) == (jnp.arange(LANES)[None, :] // HEAD_DIM)).astype(BF16) / HEAD_DIM
    outs = []
    for x in (x_prompt, x_sample):
        n_b, seq_len, _ = x.shape
        tables = _rope_tables(seq_len)
        x2 = x.reshape(n_b * seq_len, D_MODEL)
        for l in range(depth):
            x2 = _layer(x2, seq_len, tables, gmat, _layer_params(l, *weights))
        outs.append(x2.reshape(n_b, seq_len, D_MODEL))
    return tuple(outs)
```

```python
import functools
import math

import jax
import jax.numpy as jnp
from jax import lax
from jax.experimental import pallas as pl
from jax.experimental.pallas import tpu as pltpu

F32 = jnp.float32
BF16 = jnp.bfloat16

D_MODEL = 1024
GRID_W = 64
HEAD_DIM = 64
N_HEADS = 8
N_KV = 2
GROUP = N_HEADS // N_KV
ROPE_THETA = 10000.0
N_FREQ = HEAD_DIM // 4
H_RET = 4
RET_DK = 64
RET_DV = 128
RET_CHUNK = 128
D_FF = 2816
EPS = 1e-6

W_QA = N_HEADS * HEAD_DIM
W_KA = N_KV * HEAD_DIM
W_VA = N_KV * HEAD_DIM
W_QR = H_RET * RET_DK
W_KR = H_RET * RET_DK
W_VR = H_RET * RET_DV
W_GR = H_RET * RET_DV
W_GATE = 2 * D_MODEL
OFF_KA = W_QA
OFF_VA = OFF_KA + W_KA
OFF_QR = OFF_VA + W_VA
OFF_KR = OFF_QR + W_QR
OFF_VR = OFF_KR + W_KR
OFF_GR = OFF_VR + W_VR
OFF_GATE = OFF_GR + W_GR
IN_COLS = OFF_GATE + W_GATE

LANES = 128
SUBLANES = 8
BF16_ROWS = 16
LOG2E = 1.4426950408889634

TM_PROJ = 1024
TQ = 256
TKV = 512
ATTN_UNROLL = 8
BOUND_MARGIN = 1.01
SAFE_SHIFT = 60.0
VT_ROWS = HEAD_DIM + BF16_ROWS
TS_RET = 512
FC = 256
HALO = 8
VMEM_LIMIT = 56 * 1024 * 1024


def _cparams(n_axes):
    return pltpu.CompilerParams(dimension_semantics=("arbitrary",) * n_axes, vmem_limit_bytes=VMEM_LIMIT)


def _const_spec(shape):
    nd = len(shape)
    return pl.BlockSpec(shape, lambda *_: (0,) * nd, pipeline_mode=pl.Buffered(1))


def _rope_tables(seq_len):
    t = jnp.arange(seq_len, dtype=F32)
    row = jnp.floor(t / GRID_W)
    col = t - row * GRID_W
    freqs = 1.0 / (ROPE_THETA ** (jnp.arange(N_FREQ, dtype=F32) / N_FREQ))
    ang = jnp.stack([row[:, None] * freqs, col[:, None] * freqs], axis=1)
    cos = jnp.cos(ang)
    sin = jnp.sin(ang)
    zero = jnp.zeros_like(sin)
    cos_h = jnp.stack([cos, cos], axis=2).reshape(seq_len, HEAD_DIM)
    sa_h = jnp.stack([-sin, zero], axis=2).reshape(seq_len, HEAD_DIM)
    sb_h = jnp.stack([zero, sin], axis=2).reshape(seq_len, HEAD_DIM)
    tile2 = lambda a: jnp.concatenate([a, a], axis=1)
    return tile2(cos_h), tile2(sa_h), tile2(sb_h)


def _rope(x, cos, sa, sb):
    return x * cos + pltpu.roll(x, LANES - N_FREQ, 1) * sa + pltpu.roll(x, N_FREQ, 1) * sb


def _in_proj_kernel(x_ref, g1_ref, w_ref, qg_ref, kg_ref, shift_ref, bg_ref, cos_ref, sa_ref, sb_ref, gm_ref,
                    q_ref, k_ref, vt_ref, qr_ref, kr_ref, vr_ref, gr_ref, gate_ref):
    tm = x_ref.shape[0]
    x = x_ref[...]
    h = (x * lax.rsqrt(jnp.mean(x * x, axis=-1, keepdims=True) + EPS) * g1_ref[...]).astype(BF16)
    cos = cos_ref[...]
    sa = sa_ref[...]
    sb = sb_ref[...]
    lane = lax.broadcasted_iota(jnp.int32, (tm, LANES), 1)
    low = lane < HEAD_DIM

    def proj(c0, width):
        return jnp.dot(h, w_ref[:, c0:c0 + width], preferred_element_type=F32)

    def head_rms(blk):
        ss = blk * blk
        hi = ss.astype(BF16)
        lo = (ss - hi.astype(F32)).astype(BF16)
        ms = jnp.dot(jnp.concatenate([hi, lo], axis=1), gm_ref[...], preferred_element_type=F32)
        return lax.rsqrt(ms + EPS)

    qa = proj(0, W_QA)
    shift_rows = jnp.where(lax.broadcasted_iota(jnp.int32, (HEAD_DIM, LANES), 0) == 0, -shift_ref[...], 0.0)
    shift_rows = jnp.tile(shift_rows, (1, GROUP * TQ // LANES)).astype(BF16)
    for g in range(N_KV):
        for t in range(tm // TQ):
            q_ref[g, t, HEAD_DIM:, :] = shift_rows
    for b in range(GROUP):
        blk = qa[:, b * LANES:(b + 1) * LANES]
        out_t = _rope(blk * head_rms(blk) * qg_ref[...], cos, sa, sb).T
        for g in range(N_KV):
            for t in range(tm // TQ):
                q_ref[g, t, :HEAD_DIM, b * TQ:(b + 1) * TQ] = (
                    out_t[g * HEAD_DIM:(g + 1) * HEAD_DIM, t * TQ:(t + 1) * TQ].astype(BF16))

    kv = proj(OFF_KA, W_KA + W_VA)
    kblk = kv[:, :LANES]
    kout = _rope(kblk * head_rms(kblk) * kg_ref[...], cos, sa, sb)
    one_lane = jnp.where(lane == HEAD_DIM, 1.0, 0.0)
    k_ref[0] = jnp.where(low, kout, one_lane).astype(BF16)
    k_ref[1] = jnp.where(low, pltpu.roll(kout, HEAD_DIM, 1), one_lane).astype(BF16)
    vt = kv[:, LANES:].T
    ones_rows = (lax.broadcasted_iota(jnp.int32, (BF16_ROWS, TKV), 0) == 0).astype(BF16)
    for g in range(N_KV):
        for c in range(tm // TKV):
            vt_ref[g, c, :HEAD_DIM, :] = vt[g * HEAD_DIM:(g + 1) * HEAD_DIM, c * TKV:(c + 1) * TKV].astype(BF16)
            vt_ref[g, c, HEAD_DIM:, :] = ones_rows

    qr = proj(OFF_QR, W_QR)
    kr = proj(OFF_KR, W_KR)
    for b in range(H_RET // 2):
        qo = _rope(qr[:, b * LANES:(b + 1) * LANES], cos, sa, sb)
        ko = _rope(kr[:, b * LANES:(b + 1) * LANES], cos, sa, sb) * (RET_DK ** -0.5)
        qr_ref[2 * b] = jnp.where(low, qo, 0.0).astype(BF16)
        qr_ref[2 * b + 1] = jnp.where(low, 0.0, qo).astype(BF16)
        kr_ref[2 * b] = jnp.where(low, ko, 0.0).astype(BF16)
        kr_ref[2 * b + 1] = jnp.where(low, 0.0, ko).astype(BF16)

    vr_ref[...] = proj(OFF_VR, W_VR).astype(BF16)
    gr = proj(OFF_GR, W_GR)
    gr_ref[...] = (gr * jax.nn.sigmoid(gr)).astype(BF16)
    for c in range(W_GATE // 512):
        z = proj(OFF_GATE + c * 512, 512) + bg_ref[:, c * 512:(c + 1) * 512]
        gate_ref[:, c * 512:(c + 1) * 512] = jax.nn.sigmoid(z).astype(BF16)


def _in_proj(x2, seq_len, norm_g, w_in_p, qg, kg, shift, b_gate, tables, gmat):
    n = x2.shape[0]
    tm = TM_PROJ
    nt = seq_len // tm
    cos, sa, sb = tables
    tab_spec = pl.BlockSpec((tm, LANES), lambda i: (i % nt, 0))
    row = lambda width: pl.BlockSpec((tm, width), lambda i: (i, 0))
    heads = lambda nh: pl.BlockSpec((nh, tm, LANES), lambda i: (0, i, 0))
    n_b = n // seq_len
    return pl.pallas_call(
        _in_proj_kernel,
        grid=(n // tm,),
        in_specs=[row(D_MODEL), _const_spec((1, D_MODEL)), _const_spec((D_MODEL, IN_COLS)),
                  _const_spec((1, LANES)), _const_spec((1, LANES)), _const_spec((1, LANES)),
                  _const_spec((1, W_GATE)), tab_spec, tab_spec, tab_spec, _const_spec((2 * LANES, LANES))],
        out_specs=[pl.BlockSpec((N_KV, tm // TQ, LANES, GROUP * TQ), lambda i: (0, i, 0, 0)),
                   heads(N_KV),
                   pl.BlockSpec((None, N_KV, tm // TKV, VT_ROWS, TKV), lambda i: (i // nt, 0, i % nt, 0, 0)),
                   heads(H_RET), heads(H_RET), row(W_VR), row(W_GR), row(W_GATE)],
        out_shape=[jax.ShapeDtypeStruct((N_KV, n // TQ, LANES, GROUP * TQ), BF16),
                   jax.ShapeDtypeStruct((N_KV, n, LANES), BF16),
                   jax.ShapeDtypeStruct((n_b, N_KV, seq_len // TKV, VT_ROWS, TKV), BF16),
                   jax.ShapeDtypeStruct((H_RET, n, LANES), BF16),
                   jax.ShapeDtypeStruct((H_RET, n, LANES), BF16),
                   jax.ShapeDtypeStruct((n, W_VR), BF16),
                   jax.ShapeDtypeStruct((n, W_GR), BF16),
                   jax.ShapeDtypeStruct((n, W_GATE), BF16)],
        compiler_params=_cparams(1),
        name="in_proj",
    )(x2, norm_g, w_in_p, qg, kg, shift, b_gate, cos, sa, sb, gmat)


def _attn_kernel(shift_ref, q_ref, k_ref, vt_ref, o_ref, s_scr, p_scr, acc_scr, m_scr, *, n_kv):
    width = GROUP * TQ
    safe = shift_ref[0] <= SAFE_SHIFT

    def scores(j):
        off = pl.multiple_of(j * TKV, TKV)
        return jnp.dot(k_ref[pl.ds(off, TKV), :], q_ref[...], preferred_element_type=F32)

    @pl.when(safe)
    def _():
        acc_scr[...] = jnp.zeros_like(acc_scr)

        def body(j, carry):
            p = jnp.exp2(scores(j)).astype(BF16)
            acc_scr[...] += jnp.dot(vt_ref[j], p, preferred_element_type=F32)
            return carry

        lax.fori_loop(0, n_kv, body, 0, unroll=min(n_kv, ATTN_UNROLL))

    @pl.when(jnp.logical_not(safe))
    def _():
        s_scr[...] = scores(0)
        p_scr[...] = jnp.zeros_like(p_scr)
        acc_scr[...] = jnp.zeros_like(acc_scr)
        m_scr[...] = jnp.full_like(m_scr, -1e30)

        def body(j, alpha_prev):
            s_next = scores(jnp.minimum(j + 1, n_kv - 1))
            s = s_scr[...]
            m_old = m_scr[...]
            m_new = jnp.maximum(m_old, jnp.max(s, axis=0, keepdims=True))
            p = jnp.exp2(s - m_new).astype(BF16)
            alpha = jnp.exp2(m_old - m_new)
            pv = jnp.dot(vt_ref[jnp.maximum(j - 1, 0)], p_scr[...], preferred_element_type=F32)
            acc_scr[...] = acc_scr[...] * alpha_prev + pv
            p_scr[...] = p
            s_scr[...] = s_next
            m_scr[...] = m_new
            return alpha

        alpha_last = lax.fori_loop(0, n_kv, body, jnp.ones((1, width), F32))
        acc_scr[...] = acc_scr[...] * alpha_last + jnp.dot(vt_ref[n_kv - 1], p_scr[...],
                                                           preferred_element_type=F32)

    acc = acc_scr[...]
    o = acc[:HEAD_DIM] / acc[HEAD_DIM:HEAD_DIM + 1]
    z = jnp.concatenate([o[:, hh * TQ:(hh + 1) * TQ] for hh in range(GROUP)], axis=0)
    o_ref[...] = z.T.astype(o_ref.dtype)


def _attention(shift, q, k, vt, seq_len):
    n = k.shape[1]
    n_b = n // seq_len
    nq = seq_len // TQ
    n_kv = seq_len // TKV
    width = GROUP * TQ
    return pl.pallas_call(
        functools.partial(_attn_kernel, n_kv=n_kv),
        grid=(n_b, N_KV, nq),
        in_specs=[pl.BlockSpec(memory_space=pltpu.SMEM),
                  pl.BlockSpec((None, None, LANES, width), lambda b, g, i: (g, b * nq + i, 0, 0)),
                  pl.BlockSpec((None, seq_len, LANES), lambda b, g, i: (g, b, 0)),
                  pl.BlockSpec((None, None, n_kv, VT_ROWS, TKV), lambda b, g, i: (b, g, 0, 0, 0))],
        out_specs=pl.BlockSpec((TQ, GROUP * HEAD_DIM), lambda b, g, i: (b * nq + i, g)),
        out_shape=jax.ShapeDtypeStruct((n, W_QA), BF16),
        scratch_shapes=[pltpu.VMEM((TKV, width), F32), pltpu.VMEM((TKV, width), BF16),
                        pltpu.VMEM((VT_ROWS, width), F32), pltpu.VMEM((1, width), F32)],
        compiler_params=_cparams(3),
        name="attn",
    )(shift, q, k, vt)


def _decay_tables(dec_ref, h):
    lg = jax.nn.log_sigmoid(dec_ref[h])
    return jnp.broadcast_to(lg[0:1, :], (RET_CHUNK, LANES))


def _ret_fwd_kernel(q_ref, k_ref, v_ref, decf_ref, decb_ref, y_ref, r_scr):
    ts = v_ref.shape[0]
    c_len = RET_CHUNK
    n_c = ts // c_len

    @pl.when(pl.program_id(1) == 0)
    def _():
        r_scr[...] = jnp.zeros_like(r_scr)

    ri = lax.broadcasted_iota(jnp.int32, (c_len, LANES), 0).astype(F32)
    ci = lax.broadcasted_iota(jnp.int32, (c_len, LANES), 1).astype(F32)
    diff = ri - ci
    dn_t = (((1,), (1,)), ((), ()))
    dn_kv = (((0,), (0,)), ((), ()))
    blk = lambda h, c: (slice(c * c_len, (c + 1) * c_len), slice(h * RET_DV, (h + 1) * RET_DV))
    kv = {}
    for h in range(H_RET):
        lf = _decay_tables(decf_ref, h)
        lb = _decay_tables(decb_ref, h)
        dmat = jnp.where(diff >= 0, jnp.exp(lf * jnp.maximum(diff, 0.0)), jnp.exp(lb * jnp.maximum(-diff, 0.0)))
        kdec = jnp.exp(lf * (c_len - 1.0 - ri))
        for c in range(n_c):
            rows, cols = blk(h, c)
            qh = q_ref[h, rows, :]
            kh = k_ref[h, rows, :]
            vh = v_ref[rows, cols]
            s = lax.dot_general(qh, kh, dn_t, preferred_element_type=F32)
            y_ref[rows, cols] = jnp.dot((s * dmat).astype(BF16), vh, preferred_element_type=F32)
            vd = (vh.astype(F32) * kdec).astype(BF16)
            kv[h, c] = lax.dot_general(kh, vd, dn_kv, preferred_element_type=F32)
    for h in range(H_RET):
        lf = _decay_tables(decf_ref, h)
        qdec = jnp.exp(lf * (ri + 1.0))
        cdec = jnp.exp(lf * float(c_len))
        r = r_scr[h]
        for c in range(n_c):
            rows, cols = blk(h, c)
            y_ref[rows, cols] += jnp.dot(q_ref[h, rows, :], r.astype(BF16), preferred_element_type=F32) * qdec
            r = r * cdec + kv[h, c]
        r_scr[h] = r


def _ret_bwd_kernel(q_ref, k_ref, v_ref, y_ref, gr_ref, decb_ref, o_ref, r_scr):
    ts = v_ref.shape[0]
    c_len = RET_CHUNK
    n_c = ts // c_len

    @pl.when(pl.program_id(1) == 0)
    def _():
        r_scr[...] = jnp.zeros_like(r_scr)

    ri = lax.broadcasted_iota(jnp.int32, (c_len, LANES), 0).astype(F32)
    dn_kv = (((0,), (0,)), ((), ()))
    blk = lambda h, c: (slice(c * c_len, (c + 1) * c_len), slice(h * RET_DV, (h + 1) * RET_DV))
    kv = {}
    for h in range(H_RET):
        kdec = jnp.exp(_decay_tables(decb_ref, h) * ri)
        for c in range(n_c):
            rows, cols = blk(h, c)
            vd = (v_ref[rows, cols].astype(F32) * kdec).astype(BF16)
            kv[h, c] = lax.dot_general(k_ref[h, rows, :], vd, dn_kv, preferred_element_type=F32)
    for h in range(H_RET):
        lb = _decay_tables(decb_ref, h)
        qdec = jnp.exp(lb * (float(c_len) - ri))
        cdec = jnp.exp(lb * float(c_len))
        r = r_scr[h]
        for c in reversed(range(n_c)):
            rows, cols = blk(h, c)
            y = y_ref[rows, cols] + jnp.dot(q_ref[h, rows, :], r.astype(BF16), preferred_element_type=F32) * qdec
            yn = y * lax.rsqrt(jnp.mean(y * y, axis=-1, keepdims=True) + EPS)
            o_ref[rows, cols] = (gr_ref[rows, cols].astype(F32) * yn).astype(o_ref.dtype)
            r = r * cdec + kv[h, c]
        r_scr[h] = r


def _retention(qr, kr, vr, gr, decf, decb, seq_len):
    n = vr.shape[0]
    n_b = n // seq_len
    ts = TS_RET
    nt = seq_len // ts
    fwd = lambda b, j: b * nt + j
    bwd = lambda b, j: b * nt + (nt - 1 - j)
    heads = lambda f: pl.BlockSpec((H_RET, ts, LANES), lambda b, j: (0, f(b, j), 0))
    row = lambda f: pl.BlockSpec((ts, W_VR), lambda b, j: (f(b, j), 0))
    dec_spec = _const_spec((H_RET, 8, LANES))
    state = pltpu.VMEM((H_RET, LANES, RET_DV), F32)
    y1 = pl.pallas_call(
        _ret_fwd_kernel,
        grid=(n_b, nt),
        in_specs=[heads(fwd), heads(fwd), row(fwd), dec_spec, dec_spec],
        out_specs=row(fwd),
        out_shape=jax.ShapeDtypeStruct((n, W_VR), F32),
        scratch_shapes=[state],
        compiler_params=_cparams(2),
        name="ret_fwd",
    )(qr, kr, vr, decf, decb)
    return pl.pallas_call(
        _ret_bwd_kernel,
        grid=(n_b, nt),
        in_specs=[heads(bwd), heads(bwd), row(bwd), row(bwd), row(bwd), dec_spec],
        out_specs=row(bwd),
        out_shape=jax.ShapeDtypeStruct((n, W_VR), BF16),
        scratch_shapes=[state],
        compiler_params=_cparams(2),
        name="ret_bwd",
    )(qr, kr, vr, y1, gr, decb)


def _merge_kernel(x_ref, a_ref, r_ref, gate_ref, wa_ref, wr_ref, wo_ref, o_ref):
    ad = jnp.dot(a_ref[...], wa_ref[...], preferred_element_type=F32)
    rd = jnp.dot(r_ref[...], wr_ref[...], preferred_element_type=F32)
    g_a = gate_ref[:, :D_MODEL].astype(F32)
    g_r = gate_ref[:, D_MODEL:].astype(F32)
    m = (g_a * ad + g_r * rd).astype(BF16)
    o_ref[...] = x_ref[...] + jnp.dot(m, wo_ref[...], preferred_element_type=F32)


def _merge(x2, attn, ret, gates, wa, wr, wo):
    n = x2.shape[0]
    tm = TM_PROJ
    row = lambda width: pl.BlockSpec((tm, width), lambda i: (i, 0))
    return pl.pallas_call(
        _merge_kernel,
        grid=(n // tm,),
        in_specs=[row(D_MODEL), row(W_QA), row(W_VR), row(W_GATE),
                  _const_spec((W_QA, D_MODEL)), _const_spec((W_VR, D_MODEL)), _const_spec((D_MODEL, D_MODEL))],
        out_specs=row(D_MODEL),
        out_shape=jax.ShapeDtypeStruct((n, D_MODEL), F32),
        compiler_params=_cparams(1),
        name="merge",
    )(x2, attn, ret, gates, wa, wr, wo)


def _ffn_kernel(xp_ref, x_ref, xn_ref, g2_ref, wup_ref, cw_ref, cb_ref, wdn_ref, o_ref, act_scr, *, seq_len):
    tm = x_ref.shape[0]
    rows = tm + 2 * HALO
    i = pl.program_id(0)
    first = (i * tm) % seq_len == 0
    last = ((i + 1) * tm) % seq_len == 0

    def norm(x):
        return x * lax.rsqrt(jnp.mean(x * x, axis=-1, keepdims=True) + EPS) * g2_ref[...]

    hp = jnp.where(first, 0.0, norm(xp_ref[...]))
    hn = jnp.where(last, 0.0, norm(xn_ref[...]))
    x = x_ref[...]
    h = jnp.concatenate([hp, norm(x), hn], axis=0).astype(BF16)

    def conv(c0):
        u = jnp.dot(h, wup_ref[:, c0:c0 + FC], preferred_element_type=F32)
        w = cw_ref[:, c0:c0 + FC]
        down = pltpu.roll(u, 1, 0)[HALO:HALO + tm]
        up = pltpu.roll(u, rows - 1, 0)[HALO:HALO + tm]
        return down * w[0:1] + u[HALO:HALO + tm] * w[1:2] + up * w[2:3] + cb_ref[:, c0:c0 + FC]

    for f in range(D_FF // FC):
        c0 = f * FC
        val = conv(c0)
        gt = conv(D_FF + c0)
        gelu = 0.5 * gt * (1.0 + lax.erf(gt * (1.0 / math.sqrt(2.0))))
        act_scr[:, c0:c0 + FC] = (gelu * val).astype(BF16)
    o_ref[...] = x + jnp.dot(act_scr[...], wdn_ref[...], preferred_element_type=F32)


def _ffn(x2, seq_len, norm_g, w_up, conv_w, conv_b, w_down):
    n = x2.shape[0]
    tm = TM_PROJ
    hb = tm // HALO
    n_hb = n // HALO
    return pl.pallas_call(
        functools.partial(_ffn_kernel, seq_len=seq_len),
        grid=(n // tm,),
        in_specs=[pl.BlockSpec((HALO, D_MODEL), lambda i: (jnp.maximum(i * hb - 1, 0), 0)),
                  pl.BlockSpec((tm, D_MODEL), lambda i: (i, 0)),
                  pl.BlockSpec((HALO, D_MODEL), lambda i: (jnp.minimum((i + 1) * hb, n_hb - 1), 0)),
                  _const_spec((1, D_MODEL)), _const_spec((D_MODEL, 2 * D_FF)),
                  _const_spec((3, 2 * D_FF)), _const_spec((1, 2 * D_FF)), _const_spec((D_FF, D_MODEL))],
        out_specs=pl.BlockSpec((tm, D_MODEL), lambda i: (i, 0)),
        out_shape=jax.ShapeDtypeStruct((n, D_MODEL), F32),
        scratch_shapes=[pltpu.VMEM((tm, D_FF), BF16)],
        compiler_params=_cparams(1),
        name="ffn",
    )(x2, x2, x2, norm_g, w_up, conv_w, conv_b, w_down)


def _layer(x2, seq_len, tables, gmat, p):
    q, k, vt, qr, kr, vr, gr, gates = _in_proj(x2, seq_len, p["norm1_g"], p["w_in"], p["qg"], p["kg"],
                                               p["shift_row"], p["b_gate"], tables, gmat)
    attn = _attention(p["shift"], q, k, vt, seq_len)
    ret = _retention(qr, kr, vr, gr, p["decf"], p["decb"], seq_len)
    x2 = _merge(x2, attn, ret, gates, p["w_attn_o"], p["w_ret_o"], p["w_out"])
    return _ffn(x2, seq_len, p["norm2_g"], p["w_up"], p["conv_w"], p["conv_b"], p["w_down"])


def _layer_params(l, norm1_g, w_in, q_norm_g, k_norm_g, ret_decay_fwd, ret_decay_bwd, w_attn_o, w_ret_o,
                  b_gate, w_out, norm2_g, w_up, conv_w, conv_b, w_down):
    head_order = [hh for b in range(GROUP) for hh in (b, GROUP + b)]
    q_cols = jnp.asarray([hh * HEAD_DIM + d for hh in head_order for d in range(HEAD_DIM)], jnp.int32)
    w = w_in[l]
    w_p = jnp.concatenate([jnp.take(w[:, :W_QA], q_cols, axis=1), w[:, W_QA:]], axis=1).astype(BF16)
    two = lambda g: jnp.concatenate([g, g]).reshape(1, LANES)
    bcast = lambda d: jnp.broadcast_to(d.astype(F32)[:, None, None], (H_RET, 8, LANES))
    shift = (BOUND_MARGIN * HEAD_DIM * LOG2E / math.sqrt(HEAD_DIM)) * (
        jnp.max(jnp.abs(q_norm_g[l])) * jnp.max(jnp.abs(k_norm_g[l]))).astype(F32)
    return dict(
        norm1_g=norm1_g[l].reshape(1, D_MODEL), w_in=w_p,
        qg=two(q_norm_g[l]) * (LOG2E / math.sqrt(HEAD_DIM)), kg=two(k_norm_g[l]),
        shift=shift.reshape(1), shift_row=jnp.broadcast_to(shift, (1, LANES)),
        decf=bcast(ret_decay_fwd[l]), decb=bcast(ret_decay_bwd[l]),
        w_attn_o=w_attn_o[l].astype(BF16), w_ret_o=w_ret_o[l].astype(BF16),
        b_gate=b_gate[l].reshape(1, W_GATE), w_out=w_out[l].astype(BF16),
        norm2_g=norm2_g[l].reshape(1, D_MODEL), w_up=w_up[l].astype(BF16),
        conv_w=conv_w[l], conv_b=conv_b[l].reshape(1, 2 * D_FF), w_down=w_down[l].astype(BF16))


def kernel(x_prompt, x_sample, norm1_g, w_in, q_norm_g, k_norm_g, ret_decay_fwd, ret_decay_bwd, w_attn_o, w_ret_o, b_gate, w_out, norm2_g, w_up, conv_w, conv_b, w_down):
    weights = (norm1_g, w_in, q_norm_g, k_norm_g, ret_decay_fwd, ret_decay_bwd, w_attn_o, w_ret_o,
               b_gate, w_out, norm2_g, w_up, conv_w, conv_b, w_down)
    depth = w_in.shape[0]
    lane = jnp.arange(2 * LANES) % LANES
    gmat = ((lane[:, None] // HEAD_DIM) == (jnp.arange(LANES)[None, :] // HEAD_DIM)).astype(BF16) / HEAD_DIM
    outs = []
    for x in (x_prompt, x_sample):
        n_b, seq_len, _ = x.shape
        tables = _rope_tables(seq_len)
        x2 = x.reshape(n_b * seq_len, D_MODEL)
        for l in range(depth):
            x2 = _layer(x2, seq_len, tables, gmat, _layer_params(l, *weights))
        outs.append(x2.reshape(n_b, seq_len, D_MODEL))
    return tuple(outs)
```

```python
import functools
import math

import jax
import jax.numpy as jnp
from jax import lax
from jax.experimental import pallas as pl
from jax.experimental.pallas import tpu as pltpu

F32 = jnp.float32
BF16 = jnp.bfloat16

D_MODEL = 1024
GRID_W = 64
HEAD_DIM = 64
N_HEADS = 8
N_KV = 2
GROUP = N_HEADS // N_KV
ROPE_THETA = 10000.0
N_FREQ = HEAD_DIM // 4
H_RET = 4
RET_DK = 64
RET_DV = 128
RET_CHUNK = 128
D_FF = 2816
EPS = 1e-6

W_QA = N_HEADS * HEAD_DIM
W_KA = N_KV * HEAD_DIM
W_VA = N_KV * HEAD_DIM
W_QR = H_RET * RET_DK
W_KR = H_RET * RET_DK
W_VR = H_RET * RET_DV
W_GR = H_RET * RET_DV
W_GATE = 2 * D_MODEL
OFF_KA = W_QA
OFF_VA = OFF_KA + W_KA
OFF_QR = OFF_VA + W_VA
OFF_KR = OFF_QR + W_QR
OFF_VR = OFF_KR + W_KR
OFF_GR = OFF_VR + W_VR
OFF_GATE = OFF_GR + W_GR
IN_COLS = OFF_GATE + W_GATE

LANES = 128
SUBLANES = 8
BF16_ROWS = 16
LOG2E = 1.4426950408889634

TM_PROJ = 1024
TQ = 256
TKV = 512
ATTN_UNROLL = 8
BOUND_MARGIN = 1.01
SAFE_SHIFT = 60.0
VT_ROWS = HEAD_DIM + BF16_ROWS
TS_RET = 1024
FC = 256
HALO = 8
VMEM_LIMIT = 56 * 1024 * 1024


def _cparams(n_axes):
    return pltpu.CompilerParams(dimension_semantics=("arbitrary",) * n_axes, vmem_limit_bytes=VMEM_LIMIT)


def _const_spec(shape):
    nd = len(shape)
    return pl.BlockSpec(shape, lambda *_: (0,) * nd, pipeline_mode=pl.Buffered(1))


def _rope_tables(seq_len):
    t = jnp.arange(seq_len, dtype=F32)
    row = jnp.floor(t / GRID_W)
    col = t - row * GRID_W
    freqs = 1.0 / (ROPE_THETA ** (jnp.arange(N_FREQ, dtype=F32) / N_FREQ))
    ang = jnp.stack([row[:, None] * freqs, col[:, None] * freqs], axis=1)
    cos = jnp.cos(ang)
    sin = jnp.sin(ang)
    zero = jnp.zeros_like(sin)
    cos_h = jnp.stack([cos, cos], axis=2).reshape(seq_len, HEAD_DIM)
    sa_h = jnp.stack([-sin, zero], axis=2).reshape(seq_len, HEAD_DIM)
    sb_h = jnp.stack([zero, sin], axis=2).reshape(seq_len, HEAD_DIM)
    tile2 = lambda a: jnp.concatenate([a, a], axis=1)
    return tile2(cos_h), tile2(sa_h), tile2(sb_h)


def _rope(x, cos, sa, sb):
    return x * cos + pltpu.roll(x, LANES - N_FREQ, 1) * sa + pltpu.roll(x, N_FREQ, 1) * sb


def _in_proj_kernel(x_ref, g1_ref, w_ref, qg_ref, kg_ref, shift_ref, bg_ref, cos_ref, sa_ref, sb_ref, gm_ref,
                    q_ref, k_ref, vt_ref, qr_ref, kr_ref, vr_ref, gr_ref, gate_ref):
    tm = x_ref.shape[0]
    x = x_ref[...]
    h = (x * lax.rsqrt(jnp.mean(x * x, axis=-1, keepdims=True) + EPS) * g1_ref[...]).astype(BF16)
    cos = cos_ref[...]
    sa = sa_ref[...]
    sb = sb_ref[...]
    lane = lax.broadcasted_iota(jnp.int32, (tm, LANES), 1)
    low = lane < HEAD_DIM

    def proj(c0, width):
        return jnp.dot(h, w_ref[:, c0:c0 + width], preferred_element_type=F32)

    def head_rms(blk):
        ss = blk * blk
        hi = ss.astype(BF16)
        lo = (ss - hi.astype(F32)).astype(BF16)
        ms = jnp.dot(jnp.concatenate([hi, lo], axis=1), gm_ref[...], preferred_element_type=F32)
        return lax.rsqrt(ms + EPS)

    qa = proj(0, W_QA)
    shift_rows = jnp.where(lax.broadcasted_iota(jnp.int32, (HEAD_DIM, LANES), 0) == 0, -shift_ref[...], 0.0)
    shift_rows = jnp.tile(shift_rows, (1, GROUP * TQ // LANES)).astype(BF16)
    for g in range(N_KV):
        for t in range(tm // TQ):
            q_ref[g, t, HEAD_DIM:, :] = shift_rows
    for b in range(GROUP):
        blk = qa[:, b * LANES:(b + 1) * LANES]
        out_t = _rope(blk * head_rms(blk) * qg_ref[...], cos, sa, sb).T
        for g in range(N_KV):
            for t in range(tm // TQ):
                q_ref[g, t, :HEAD_DIM, b * TQ:(b + 1) * TQ] = (
                    out_t[g * HEAD_DIM:(g + 1) * HEAD_DIM, t * TQ:(t + 1) * TQ].astype(BF16))

    kv = proj(OFF_KA, W_KA + W_VA)
    kblk = kv[:, :LANES]
    kout = _rope(kblk * head_rms(kblk) * kg_ref[...], cos, sa, sb)
    one_lane = jnp.where(lane == HEAD_DIM, 1.0, 0.0)
    k_ref[0] = jnp.where(low, kout, one_lane).astype(BF16)
    k_ref[1] = jnp.where(low, pltpu.roll(kout, HEAD_DIM, 1), one_lane).astype(BF16)
    vt = kv[:, LANES:].T
    ones_rows = (lax.broadcasted_iota(jnp.int32, (BF16_ROWS, TKV), 0) == 0).astype(BF16)
    for g in range(N_KV):
        for c in range(tm // TKV):
            vt_ref[g, c, :HEAD_DIM, :] = vt[g * HEAD_DIM:(g + 1) * HEAD_DIM, c * TKV:(c + 1) * TKV].astype(BF16)
            vt_ref[g, c, HEAD_DIM:, :] = ones_rows

    qr = proj(OFF_QR, W_QR)
    kr = proj(OFF_KR, W_KR)
    for b in range(H_RET // 2):
        qo = _rope(qr[:, b * LANES:(b + 1) * LANES], cos, sa, sb)
        ko = _rope(kr[:, b * LANES:(b + 1) * LANES], cos, sa, sb) * (RET_DK ** -0.5)
        qr_ref[2 * b] = jnp.where(low, qo, 0.0).astype(BF16)
        qr_ref[2 * b + 1] = jnp.where(low, 0.0, qo).astype(BF16)
        kr_ref[2 * b] = jnp.where(low, ko, 0.0).astype(BF16)
        kr_ref[2 * b + 1] = jnp.where(low, 0.0, ko).astype(BF16)

    vr_ref[...] = proj(OFF_VR, W_VR).astype(BF16)
    gr = proj(OFF_GR, W_GR)
    gr_ref[...] = (gr * jax.nn.sigmoid(gr)).astype(BF16)
    for c in range(W_GATE // 512):
        z = proj(OFF_GATE + c * 512, 512) + bg_ref[:, c * 512:(c + 1) * 512]
        gate_ref[:, c * 512:(c + 1) * 512] = jax.nn.sigmoid(z).astype(BF16)


def _in_proj(x2, seq_len, norm_g, w_in_p, qg, kg, shift, b_gate, tables, gmat):
    n = x2.shape[0]
    tm = TM_PROJ
    nt = seq_len // tm
    cos, sa, sb = tables
    tab_spec = pl.BlockSpec((tm, LANES), lambda i: (i % nt, 0))
    row = lambda width: pl.BlockSpec((tm, width), lambda i: (i, 0))
    heads = lambda nh: pl.BlockSpec((nh, tm, LANES), lambda i: (0, i, 0))
    n_b = n // seq_len
    return pl.pallas_call(
        _in_proj_kernel,
        grid=(n // tm,),
        in_specs=[row(D_MODEL), _const_spec((1, D_MODEL)), _const_spec((D_MODEL, IN_COLS)),
                  _const_spec((1, LANES)), _const_spec((1, LANES)), _const_spec((1, LANES)),
                  _const_spec((1, W_GATE)), tab_spec, tab_spec, tab_spec, _const_spec((2 * LANES, LANES))],
        out_specs=[pl.BlockSpec((N_KV, tm // TQ, LANES, GROUP * TQ), lambda i: (0, i, 0, 0)),
                   heads(N_KV),
                   pl.BlockSpec((None, N_KV, tm // TKV, VT_ROWS, TKV), lambda i: (i // nt, 0, i % nt, 0, 0)),
                   heads(H_RET), heads(H_RET), row(W_VR), row(W_GR), row(W_GATE)],
        out_shape=[jax.ShapeDtypeStruct((N_KV, n // TQ, LANES, GROUP * TQ), BF16),
                   jax.ShapeDtypeStruct((N_KV, n, LANES), BF16),
                   jax.ShapeDtypeStruct((n_b, N_KV, seq_len // TKV, VT_ROWS, TKV), BF16),
                   jax.ShapeDtypeStruct((H_RET, n, LANES), BF16),
                   jax.ShapeDtypeStruct((H_RET, n, LANES), BF16),
                   jax.ShapeDtypeStruct((n, W_VR), BF16),
                   jax.ShapeDtypeStruct((n, W_GR), BF16),
                   jax.ShapeDtypeStruct((n, W_GATE), BF16)],
        compiler_params=_cparams(1),
        name="in_proj",
    )(x2, norm_g, w_in_p, qg, kg, shift, b_gate, cos, sa, sb, gmat)


def _attn_kernel(shift_ref, q_ref, k_ref, vt_ref, o_ref, s_scr, p_scr, acc_scr, m_scr, *, n_kv):
    width = GROUP * TQ
    safe = shift_ref[0] <= SAFE_SHIFT

    def scores(j):
        off = pl.multiple_of(j * TKV, TKV)
        return jnp.dot(k_ref[pl.ds(off, TKV), :], q_ref[...], preferred_element_type=F32)

    @pl.when(safe)
    def _():
        acc_scr[...] = jnp.zeros_like(acc_scr)

        def body(j, carry):
            p = jnp.exp2(scores(j)).astype(BF16)
            acc_scr[...] += jnp.dot(vt_ref[j], p, preferred_element_type=F32)
            return carry

        lax.fori_loop(0, n_kv, body, 0, unroll=min(n_kv, ATTN_UNROLL))

    @pl.when(jnp.logical_not(safe))
    def _():
        s_scr[...] = scores(0)
        p_scr[...] = jnp.zeros_like(p_scr)
        acc_scr[...] = jnp.zeros_like(acc_scr)
        m_scr[...] = jnp.full_like(m_scr, -1e30)

        def body(j, alpha_prev):
            s_next = scores(jnp.minimum(j + 1, n_kv - 1))
            s = s_scr[...]
            m_old = m_scr[...]
            m_new = jnp.maximum(m_old, jnp.max(s, axis=0, keepdims=True))
            p = jnp.exp2(s - m_new).astype(BF16)
            alpha = jnp.exp2(m_old - m_new)
            pv = jnp.dot(vt_ref[jnp.maximum(j - 1, 0)], p_scr[...], preferred_element_type=F32)
            acc_scr[...] = acc_scr[...] * alpha_prev + pv
            p_scr[...] = p
            s_scr[...] = s_next
            m_scr[...] = m_new
            return alpha

        alpha_last = lax.fori_loop(0, n_kv, body, jnp.ones((1, width), F32))
        acc_scr[...] = acc_scr[...] * alpha_last + jnp.dot(vt_ref[n_kv - 1], p_scr[...],
                                                           preferred_element_type=F32)

    acc = acc_scr[...]
    o = acc[:HEAD_DIM] / acc[HEAD_DIM:HEAD_DIM + 1]
    z = jnp.concatenate([o[:, hh * TQ:(hh + 1) * TQ] for hh in range(GROUP)], axis=0)
    o_ref[...] = z.T.astype(o_ref.dtype)


def _attention(shift, q, k, vt, seq_len):
    n = k.shape[1]
    n_b = n // seq_len
    nq = seq_len // TQ
    n_kv = seq_len // TKV
    width = GROUP * TQ
    return pl.pallas_call(
        functools.partial(_attn_kernel, n_kv=n_kv),
        grid=(n_b, N_KV, nq),
        in_specs=[pl.BlockSpec(memory_space=pltpu.SMEM),
                  pl.BlockSpec((None, None, LANES, width), lambda b, g, i: (g, b * nq + i, 0, 0)),
                  pl.BlockSpec((None, seq_len, LANES), lambda b, g, i: (g, b, 0)),
                  pl.BlockSpec((None, None, n_kv, VT_ROWS, TKV), lambda b, g, i: (b, g, 0, 0, 0))],
        out_specs=pl.BlockSpec((TQ, GROUP * HEAD_DIM), lambda b, g, i: (b * nq + i, g)),
        out_shape=jax.ShapeDtypeStruct((n, W_QA), BF16),
        scratch_shapes=[pltpu.VMEM((TKV, width), F32), pltpu.VMEM((TKV, width), BF16),
                        pltpu.VMEM((VT_ROWS, width), F32), pltpu.VMEM((1, width), F32)],
        compiler_params=_cparams(3),
        name="attn",
    )(shift, q, k, vt)


def _decay_tables(dec_ref, h):
    lg = jax.nn.log_sigmoid(dec_ref[h])
    return jnp.broadcast_to(lg[0:1, :], (RET_CHUNK, LANES))


def _ret_fwd_kernel(q_ref, k_ref, v_ref, decf_ref, decb_ref, y_ref, r_scr):
    ts = v_ref.shape[0]
    c_len = RET_CHUNK
    n_c = ts // c_len

    @pl.when(pl.program_id(1) == 0)
    def _():
        r_scr[...] = jnp.zeros_like(r_scr)

    ri = lax.broadcasted_iota(jnp.int32, (c_len, LANES), 0).astype(F32)
    ci = lax.broadcasted_iota(jnp.int32, (c_len, LANES), 1).astype(F32)
    diff = ri - ci
    dn_t = (((1,), (1,)), ((), ()))
    dn_kv = (((0,), (0,)), ((), ()))
    blk = lambda h, c: (slice(c * c_len, (c + 1) * c_len), slice(h * RET_DV, (h + 1) * RET_DV))
    kv = {}
    for h in range(H_RET):
        lf = _decay_tables(decf_ref, h)
        lb = _decay_tables(decb_ref, h)
        dmat = jnp.where(diff >= 0, jnp.exp(lf * jnp.maximum(diff, 0.0)), jnp.exp(lb * jnp.maximum(-diff, 0.0)))
        kdec = jnp.exp(lf * (c_len - 1.0 - ri))
        for c in range(n_c):
            rows, cols = blk(h, c)
            qh = q_ref[h, rows, :]
            kh = k_ref[h, rows, :]
            vh = v_ref[rows, cols]
            s = lax.dot_general(qh, kh, dn_t, preferred_element_type=F32)
            y_ref[rows, cols] = jnp.dot((s * dmat).astype(BF16), vh, preferred_element_type=F32)
            vd = (vh.astype(F32) * kdec).astype(BF16)
            kv[h, c] = lax.dot_general(kh, vd, dn_kv, preferred_element_type=F32)
    for h in range(H_RET):
        lf = _decay_tables(decf_ref, h)
        qdec = jnp.exp(lf * (ri + 1.0))
        cdec = jnp.exp(lf * float(c_len))
        r = r_scr[h]
        for c in range(n_c):
            rows, cols = blk(h, c)
            y_ref[rows, cols] += jnp.dot(q_ref[h, rows, :], r.astype(BF16), preferred_element_type=F32) * qdec
            r = r * cdec + kv[h, c]
        r_scr[h] = r


def _ret_bwd_kernel(q_ref, k_ref, v_ref, y_ref, gr_ref, decb_ref, o_ref, r_scr):
    ts = v_ref.shape[0]
    c_len = RET_CHUNK
    n_c = ts // c_len

    @pl.when(pl.program_id(1) == 0)
    def _():
        r_scr[...] = jnp.zeros_like(r_scr)

    ri = lax.broadcasted_iota(jnp.int32, (c_len, LANES), 0).astype(F32)
    dn_kv = (((0,), (0,)), ((), ()))
    blk = lambda h, c: (slice(c * c_len, (c + 1) * c_len), slice(h * RET_DV, (h + 1) * RET_DV))
    kv = {}
    for h in range(H_RET):
        kdec = jnp.exp(_decay_tables(decb_ref, h) * ri)
        for c in range(n_c):
            rows, cols = blk(h, c)
            vd = (v_ref[rows, cols].astype(F32) * kdec).astype(BF16)
            kv[h, c] = lax.dot_general(k_ref[h, rows, :], vd, dn_kv, preferred_element_type=F32)
    for h in range(H_RET):
        lb = _decay_tables(decb_ref, h)
        qdec = jnp.exp(lb * (float(c_len) - ri))
        cdec = jnp.exp(lb * float(c_len))
        r = r_scr[h]
        for c in reversed(range(n_c)):
            rows, cols = blk(h, c)
            y = y_ref[rows, cols] + jnp.dot(q_ref[h, rows, :], r.astype(BF16), preferred_element_type=F32) * qdec
            yn = y * lax.rsqrt(jnp.mean(y * y, axis=-1, keepdims=True) + EPS)
            o_ref[rows, cols] = (gr_ref[rows, cols].astype(F32) * yn).astype(o_ref.dtype)
            r = r * cdec + kv[h, c]
        r_scr[h] = r


def _retention(qr, kr, vr, gr, decf, decb, seq_len):
    n = vr.shape[0]
    n_b = n // seq_len
    ts = TS_RET
    nt = seq_len // ts
    fwd = lambda b, j: b * nt + j
    bwd = lambda b, j: b * nt + (nt - 1 - j)
    heads = lambda f: pl.BlockSpec((H_RET, ts, LANES), lambda b, j: (0, f(b, j), 0))
    row = lambda f: pl.BlockSpec((ts, W_VR), lambda b, j: (f(b, j), 0))
    dec_spec = _const_spec((H_RET, 8, LANES))
    state = pltpu.VMEM((H_RET, LANES, RET_DV), F32)
    y1 = pl.pallas_call(
        _ret_fwd_kernel,
        grid=(n_b, nt),
        in_specs=[heads(fwd), heads(fwd), row(fwd), dec_spec, dec_spec],
        out_specs=row(fwd),
        out_shape=jax.ShapeDtypeStruct((n, W_VR), F32),
        scratch_shapes=[state],
        compiler_params=_cparams(2),
        name="ret_fwd",
    )(qr, kr, vr, decf, decb)
    return pl.pallas_call(
        _ret_bwd_kernel,
        grid=(n_b, nt),
        in_specs=[heads(bwd), heads(bwd), row(bwd), row(bwd), row(bwd), dec_spec],
        out_specs=row(bwd),
        out_shape=jax.ShapeDtypeStruct((n, W_VR), BF16),
        scratch_shapes=[state],
        compiler_params=_cparams(2),
        name="ret_bwd",
    )(qr, kr, vr, y1, gr, decb)


def _merge_kernel(x_ref, a_ref, r_ref, gate_ref, wa_ref, wr_ref, wo_ref, o_ref):
    ad = jnp.dot(a_ref[...], wa_ref[...], preferred_element_type=F32)
    rd = jnp.dot(r_ref[...], wr_ref[...], preferred_element_type=F32)
    g_a = gate_ref[:, :D_MODEL].astype(F32)
    g_r = gate_ref[:, D_MODEL:].astype(F32)
    m = (g_a * ad + g_r * rd).astype(BF16)
    o_ref[...] = x_ref[...] + jnp.dot(m, wo_ref[...], preferred_element_type=F32)


def _merge(x2, attn, ret, gates, wa, wr, wo):
    n = x2.shape[0]
    tm = TM_PROJ
    row = lambda width: pl.BlockSpec((tm, width), lambda i: (i, 0))
    return pl.pallas_call(
        _merge_kernel,
        grid=(n // tm,),
        in_specs=[row(D_MODEL), row(W_QA), row(W_VR), row(W_GATE),
                  _const_spec((W_QA, D_MODEL)), _const_spec((W_VR, D_MODEL)), _const_spec((D_MODEL, D_MODEL))],
        out_specs=row(D_MODEL),
        out_shape=jax.ShapeDtypeStruct((n, D_MODEL), F32),
        compiler_params=_cparams(1),
        name="merge",
    )(x2, attn, ret, gates, wa, wr, wo)


def _ffn_kernel(xp_ref, x_ref, xn_ref, g2_ref, wup_ref, cw_ref, cb_ref, wdn_ref, o_ref, act_scr, *, seq_len):
    tm = x_ref.shape[0]
    rows = tm + 2 * HALO
    i = pl.program_id(0)
    first = (i * tm) % seq_len == 0
    last = ((i + 1) * tm) % seq_len == 0

    def norm(x):
        return x * lax.rsqrt(jnp.mean(x * x, axis=-1, keepdims=True) + EPS) * g2_ref[...]

    hp = jnp.where(first, 0.0, norm(xp_ref[...]))
    hn = jnp.where(last, 0.0, norm(xn_ref[...]))
    x = x_ref[...]
    h = jnp.concatenate([hp, norm(x), hn], axis=0).astype(BF16)

    def conv(c0):
        u = jnp.dot(h, wup_ref[:, c0:c0 + FC], preferred_element_type=F32)
        w = cw_ref[:, c0:c0 + FC]
        down = pltpu.roll(u, 1, 0)[HALO:HALO + tm]
        up = pltpu.roll(u, rows - 1, 0)[HALO:HALO + tm]
        return down * w[0:1] + u[HALO:HALO + tm] * w[1:2] + up * w[2:3] + cb_ref[:, c0:c0 + FC]

    for f in range(D_FF // FC):
        c0 = f * FC
        val = conv(c0)
        gt = conv(D_FF + c0)
        gelu = 0.5 * gt * (1.0 + lax.erf(gt * (1.0 / math.sqrt(2.0))))
        act_scr[:, c0:c0 + FC] = (gelu * val).astype(BF16)
    o_ref[...] = x + jnp.dot(act_scr[...], wdn_ref[...], preferred_element_type=F32)


def _ffn(x2, seq_len, norm_g, w_up, conv_w, conv_b, w_down):
    n = x2.shape[0]
    tm = TM_PROJ
    hb = tm // HALO
    n_hb = n // HALO
    return pl.pallas_call(
        functools.partial(_ffn_kernel, seq_len=seq_len),
        grid=(n // tm,),
        in_specs=[pl.BlockSpec((HALO, D_MODEL), lambda i: (jnp.maximum(i * hb - 1, 0), 0)),
                  pl.BlockSpec((tm, D_MODEL), lambda i: (i, 0)),
                  pl.BlockSpec((HALO, D_MODEL), lambda i: (jnp.minimum((i + 1) * hb, n_hb - 1), 0)),
                  _const_spec((1, D_MODEL)), _const_spec((D_MODEL, 2 * D_FF)),
                  _const_spec((3, 2 * D_FF)), _const_spec((1, 2 * D_FF)), _const_spec((D_FF, D_MODEL))],
        out_specs=pl.BlockSpec((tm, D_MODEL), lambda i: (i, 0)),
        out_shape=jax.ShapeDtypeStruct((n, D_MODEL), F32),
        scratch_shapes=[pltpu.VMEM((tm, D_FF), BF16)],
        compiler_params=_cparams(1),
        name="ffn",
    )(x2, x2, x2, norm_g, w_up, conv_w, conv_b, w_down)


def _layer(x2, seq_len, tables, gmat, p):
    q, k, vt, qr, kr, vr, gr, gates = _in_proj(x2, seq_len, p["norm1_g"], p["w_in"], p["qg"], p["kg"],
                                               p["shift_row"], p["b_gate"], tables, gmat)
    attn = _attention(p["shift"], q, k, vt, seq_len)
    ret = _retention(qr, kr, vr, gr, p["decf"], p["decb"], seq_len)
    x2 = _merge(x2, attn, ret, gates, p["w_attn_o"], p["w_ret_o"], p["w_out"])
    return _ffn(x2, seq_len, p["norm2_g"], p["w_up"], p["conv_w"], p["conv_b"], p["w_down"])


def _layer_params(l, norm1_g, w_in, q_norm_g, k_norm_g, ret_decay_fwd, ret_decay_bwd, w_attn_o, w_ret_o,
                  b_gate, w_out, norm2_g, w_up, conv_w, conv_b, w_down):
    head_order = [hh for b in range(GROUP) for hh in (b, GROUP + b)]
    q_cols = jnp.asarray([hh * HEAD_DIM + d for hh in head_order for d in range(HEAD_DIM)], jnp.int32)
    w = w_in[l]
    w_p = jnp.concatenate([jnp.take(w[:, :W_QA], q_cols, axis=1), w[:, W_QA:]], axis=1).astype(BF16)
    two = lambda g: jnp.concatenate([g, g]).reshape(1, LANES)
    bcast = lambda d: jnp.broadcast_to(d.astype(F32)[:, None, None], (H_RET, 8, LANES))
    shift = (BOUND_MARGIN * HEAD_DIM * LOG2E / math.sqrt(HEAD_DIM)) * (
        jnp.max(jnp.abs(q_norm_g[l])) * jnp.max(jnp.abs(k_norm_g[l]))).astype(F32)
    return dict(
        norm1_g=norm1_g[l].reshape(1, D_MODEL), w_in=w_p,
        qg=two(q_norm_g[l]) * (LOG2E / math.sqrt(HEAD_DIM)), kg=two(k_norm_g[l]),
        shift=shift.reshape(1), shift_row=jnp.broadcast_to(shift, (1, LANES)),
        decf=bcast(ret_decay_fwd[l]), decb=bcast(ret_decay_bwd[l]),
        w_attn_o=w_attn_o[l].astype(BF16), w_ret_o=w_ret_o[l].astype(BF16),
        b_gate=b_gate[l].reshape(1, W_GATE), w_out=w_out[l].astype(BF16),
        norm2_g=norm2_g[l].reshape(1, D_MODEL), w_up=w_up[l].astype(BF16),
        conv_w=conv_w[l], conv_b=conv_b[l].reshape(1, 2 * D_FF), w_down=w_down[l].astype(BF16))


def kernel(x_prompt, x_sample, norm1_g, w_in, q_norm_g, k_norm_g, ret_decay_fwd, ret_decay_bwd, w_attn_o, w_ret_o, b_gate, w_out, norm2_g, w_up, conv_w, conv_b, w_down):
    weights = (norm1_g, w_in, q_norm_g, k_norm_g, ret_decay_fwd, ret_decay_bwd, w_attn_o, w_ret_o,
               b_gate, w_out, norm2_g, w_up, conv_w, conv_b, w_down)
    depth = w_in.shape[0]
    lane = jnp.arange(2 * LANES) % LANES
    gmat = ((lane[:, None] // HEAD_DIM) == (jnp.arange(LANES)[None, :] // HEAD_DIM)).astype(BF16) / HEAD_DIM
    outs = []
    for x in (x_prompt, x_sample):
        n_b, seq_len, _ = x.shape
        tables = _rope_tables(seq_len)
        x2 = x.reshape(n_b * seq_len, D_MODEL)
        for l in range(depth):
            x2 = _layer(x2, seq_len, tables, gmat, _layer_params(l, *weights))
        outs.append(x2.reshape(n_b, seq_len, D_MODEL))
    return tuple(outs)
```

```python
import functools
import math

import jax
import jax.numpy as jnp
from jax import lax
from jax.experimental import pallas as pl
from jax.experimental.pallas import tpu as pltpu

F32 = jnp.float32
BF16 = jnp.bfloat16

D_MODEL = 1024
GRID_W = 64
HEAD_DIM = 64
N_HEADS = 8
N_KV = 2
GROUP = N_HEADS // N_KV
ROPE_THETA = 10000.0
N_FREQ = HEAD_DIM // 4
H_RET = 4
RET_DK = 64
RET_DV = 128
RET_CHUNK = 128
D_FF = 2816
EPS = 1e-6

W_QA = N_HEADS * HEAD_DIM
W_KA = N_KV * HEAD_DIM
W_VA = N_KV * HEAD_DIM
W_QR = H_RET * RET_DK
W_KR = H_RET * RET_DK
W_VR = H_RET * RET_DV
W_GR = H_RET * RET_DV
W_GATE = 2 * D_MODEL
OFF_KA = W_QA
OFF_VA = OFF_KA + W_KA
OFF_QR = OFF_VA + W_VA
OFF_KR = OFF_QR + W_QR
OFF_VR = OFF_KR + W_KR
OFF_GR = OFF_VR + W_VR
OFF_GATE = OFF_GR + W_GR
IN_COLS = OFF_GATE + W_GATE

LANES = 128
SUBLANES = 8
BF16_ROWS = 16
LOG2E = 1.4426950408889634

TM_PROJ = 1024
TQ = 256
TKV = 512
ATTN_UNROLL = 8
BOUND_MARGIN = 1.01
SAFE_SHIFT = 60.0
VT_ROWS = HEAD_DIM + BF16_ROWS
TS_RET = 2048
FC = 256
HALO = 8
VMEM_LIMIT = 56 * 1024 * 1024


def _cparams(n_axes):
    return pltpu.CompilerParams(dimension_semantics=("arbitrary",) * n_axes, vmem_limit_bytes=VMEM_LIMIT)


def _const_spec(shape):
    nd = len(shape)
    return pl.BlockSpec(shape, lambda *_: (0,) * nd, pipeline_mode=pl.Buffered(1))


def _rope_tables(seq_len):
    t = jnp.arange(seq_len, dtype=F32)
    row = jnp.floor(t / GRID_W)
    col = t - row * GRID_W
    freqs = 1.0 / (ROPE_THETA ** (jnp.arange(N_FREQ, dtype=F32) / N_FREQ))
    ang = jnp.stack([row[:, None] * freqs, col[:, None] * freqs], axis=1)
    cos = jnp.cos(ang)
    sin = jnp.sin(ang)
    zero = jnp.zeros_like(sin)
    cos_h = jnp.stack([cos, cos], axis=2).reshape(seq_len, HEAD_DIM)
    sa_h = jnp.stack([-sin, zero], axis=2).reshape(seq_len, HEAD_DIM)
    sb_h = jnp.stack([zero, sin], axis=2).reshape(seq_len, HEAD_DIM)
    tile2 = lambda a: jnp.concatenate([a, a], axis=1)
    return tile2(cos_h), tile2(sa_h), tile2(sb_h)


def _rope(x, cos, sa, sb):
    return x * cos + pltpu.roll(x, LANES - N_FREQ, 1) * sa + pltpu.roll(x, N_FREQ, 1) * sb


def _in_proj_kernel(x_ref, g1_ref, w_ref, qg_ref, kg_ref, shift_ref, bg_ref, cos_ref, sa_ref, sb_ref, gm_ref,
                    q_ref, k_ref, vt_ref, qr_ref, kr_ref, vr_ref, gr_ref, gate_ref):
    tm = x_ref.shape[0]
    x = x_ref[...]
    h = (x * lax.rsqrt(jnp.mean(x * x, axis=-1, keepdims=True) + EPS) * g1_ref[...]).astype(BF16)
    cos = cos_ref[...]
    sa = sa_ref[...]
    sb = sb_ref[...]
    lane = lax.broadcasted_iota(jnp.int32, (tm, LANES), 1)
    low = lane < HEAD_DIM

    def proj(c0, width):
        return jnp.dot(h, w_ref[:, c0:c0 + width], preferred_element_type=F32)

    def head_rms(blk):
        ss = blk * blk
        hi = ss.astype(BF16)
        lo = (ss - hi.astype(F32)).astype(BF16)
        ms = jnp.dot(jnp.concatenate([hi, lo], axis=1), gm_ref[...], preferred_element_type=F32)
        return lax.rsqrt(ms + EPS)

    qa = proj(0, W_QA)
    shift_rows = jnp.where(lax.broadcasted_iota(jnp.int32, (HEAD_DIM, LANES), 0) == 0, -shift_ref[...], 0.0)
    shift_rows = jnp.tile(shift_rows, (1, GROUP * TQ // LANES)).astype(BF16)
    for g in range(N_KV):
        for t in range(tm // TQ):
            q_ref[g, t, HEAD_DIM:, :] = shift_rows
    for b in range(GROUP):
        blk = qa[:, b * LANES:(b + 1) * LANES]
        out_t = _rope(blk * head_rms(blk) * qg_ref[...], cos, sa, sb).T
        for g in range(N_KV):
            for t in range(tm // TQ):
                q_ref[g, t, :HEAD_DIM, b * TQ:(b + 1) * TQ] = (
                    out_t[g * HEAD_DIM:(g + 1) * HEAD_DIM, t * TQ:(t + 1) * TQ].astype(BF16))

    kv = proj(OFF_KA, W_KA + W_VA)
    kblk = kv[:, :LANES]
    kout = _rope(kblk * head_rms(kblk) * kg_ref[...], cos, sa, sb)
    one_lane = jnp.where(lane == HEAD_DIM, 1.0, 0.0)
    k_ref[0] = jnp.where(low, kout, one_lane).astype(BF16)
    k_ref[1] = jnp.where(low, pltpu.roll(kout, HEAD_DIM, 1), one_lane).astype(BF16)
    vt = kv[:, LANES:].T
    ones_rows = (lax.broadcasted_iota(jnp.int32, (BF16_ROWS, TKV), 0) == 0).astype(BF16)
    for g in range(N_KV):
        for c in range(tm // TKV):
            vt_ref[g, c, :HEAD_DIM, :] = vt[g * HEAD_DIM:(g + 1) * HEAD_DIM, c * TKV:(c + 1) * TKV].astype(BF16)
            vt_ref[g, c, HEAD_DIM:, :] = ones_rows

    qr = proj(OFF_QR, W_QR)
    kr = proj(OFF_KR, W_KR)
    for b in range(H_RET // 2):
        qo = _rope(qr[:, b * LANES:(b + 1) * LANES], cos, sa, sb)
        ko = _rope(kr[:, b * LANES:(b + 1) * LANES], cos, sa, sb) * (RET_DK ** -0.5)
        qr_ref[2 * b] = jnp.where(low, qo, 0.0).astype(BF16)
        qr_ref[2 * b + 1] = jnp.where(low, 0.0, qo).astype(BF16)
        kr_ref[2 * b] = jnp.where(low, ko, 0.0).astype(BF16)
        kr_ref[2 * b + 1] = jnp.where(low, 0.0, ko).astype(BF16)

    vr_ref[...] = proj(OFF_VR, W_VR).astype(BF16)
    gr = proj(OFF_GR, W_GR)
    gr_ref[...] = (gr * jax.nn.sigmoid(gr)).astype(BF16)
    for c in range(W_GATE // 512):
        z = proj(OFF_GATE + c * 512, 512) + bg_ref[:, c * 512:(c + 1) * 512]
        gate_ref[:, c * 512:(c + 1) * 512] = jax.nn.sigmoid(z).astype(BF16)


def _in_proj(x2, seq_len, norm_g, w_in_p, qg, kg, shift, b_gate, tables, gmat):
    n = x2.shape[0]
    tm = TM_PROJ
    nt = seq_len // tm
    cos, sa, sb = tables
    tab_spec = pl.BlockSpec((tm, LANES), lambda i: (i % nt, 0))
    row = lambda width: pl.BlockSpec((tm, width), lambda i: (i, 0))
    heads = lambda nh: pl.BlockSpec((nh, tm, LANES), lambda i: (0, i, 0))
    n_b = n // seq_len
    return pl.pallas_call(
        _in_proj_kernel,
        grid=(n // tm,),
        in_specs=[row(D_MODEL), _const_spec((1, D_MODEL)), _const_spec((D_MODEL, IN_COLS)),
                  _const_spec((1, LANES)), _const_spec((1, LANES)), _const_spec((1, LANES)),
                  _const_spec((1, W_GATE)), tab_spec, tab_spec, tab_spec, _const_spec((2 * LANES, LANES))],
        out_specs=[pl.BlockSpec((N_KV, tm // TQ, LANES, GROUP * TQ), lambda i: (0, i, 0, 0)),
                   heads(N_KV),
                   pl.BlockSpec((None, N_KV, tm // TKV, VT_ROWS, TKV), lambda i: (i // nt, 0, i % nt, 0, 0)),
                   heads(H_RET), heads(H_RET), row(W_VR), row(W_GR), row(W_GATE)],
        out_shape=[jax.ShapeDtypeStruct((N_KV, n // TQ, LANES, GROUP * TQ), BF16),
                   jax.ShapeDtypeStruct((N_KV, n, LANES), BF16),
                   jax.ShapeDtypeStruct((n_b, N_KV, seq_len // TKV, VT_ROWS, TKV), BF16),
                   jax.ShapeDtypeStruct((H_RET, n, LANES), BF16),
                   jax.ShapeDtypeStruct((H_RET, n, LANES), BF16),
                   jax.ShapeDtypeStruct((n, W_VR), BF16),
                   jax.ShapeDtypeStruct((n, W_GR), BF16),
                   jax.ShapeDtypeStruct((n, W_GATE), BF16)],
        compiler_params=_cparams(1),
        name="in_proj",
    )(x2, norm_g, w_in_p, qg, kg, shift, b_gate, cos, sa, sb, gmat)


def _attn_kernel(shift_ref, q_ref, k_ref, vt_ref, o_ref, s_scr, p_scr, acc_scr, m_scr, *, n_kv):
    width = GROUP * TQ
    safe = shift_ref[0] <= SAFE_SHIFT

    def scores(j):
        off = pl.multiple_of(j * TKV, TKV)
        return jnp.dot(k_ref[pl.ds(off, TKV), :], q_ref[...], preferred_element_type=F32)

    @pl.when(safe)
    def _():
        acc_scr[...] = jnp.zeros_like(acc_scr)

        def body(j, carry):
            p = jnp.exp2(scores(j)).astype(BF16)
            acc_scr[...] += jnp.dot(vt_ref[j], p, preferred_element_type=F32)
            return carry

        lax.fori_loop(0, n_kv, body, 0, unroll=min(n_kv, ATTN_UNROLL))

    @pl.when(jnp.logical_not(safe))
    def _():
        s_scr[...] = scores(0)
        p_scr[...] = jnp.zeros_like(p_scr)
        acc_scr[...] = jnp.zeros_like(acc_scr)
        m_scr[...] = jnp.full_like(m_scr, -1e30)

        def body(j, alpha_prev):
            s_next = scores(jnp.minimum(j + 1, n_kv - 1))
            s = s_scr[...]
            m_old = m_scr[...]
            m_new = jnp.maximum(m_old, jnp.max(s, axis=0, keepdims=True))
            p = jnp.exp2(s - m_new).astype(BF16)
            alpha = jnp.exp2(m_old - m_new)
            pv = jnp.dot(vt_ref[jnp.maximum(j - 1, 0)], p_scr[...], preferred_element_type=F32)
            acc_scr[...] = acc_scr[...] * alpha_prev + pv
            p_scr[...] = p
            s_scr[...] = s_next
            m_scr[...] = m_new
            return alpha

        alpha_last = lax.fori_loop(0, n_kv, body, jnp.ones((1, width), F32))
        acc_scr[...] = acc_scr[...] * alpha_last + jnp.dot(vt_ref[n_kv - 1], p_scr[...],
                                                           preferred_element_type=F32)

    acc = acc_scr[...]
    o = acc[:HEAD_DIM] / acc[HEAD_DIM:HEAD_DIM + 1]
    z = jnp.concatenate([o[:, hh * TQ:(hh + 1) * TQ] for hh in range(GROUP)], axis=0)
    o_ref[...] = z.T.astype(o_ref.dtype)


def _attention(shift, q, k, vt, seq_len):
    n = k.shape[1]
    n_b = n // seq_len
    nq = seq_len // TQ
    n_kv = seq_len // TKV
    width = GROUP * TQ
    return pl.pallas_call(
        functools.partial(_attn_kernel, n_kv=n_kv),
        grid=(n_b, N_KV, nq),
        in_specs=[pl.BlockSpec(memory_space=pltpu.SMEM),
                  pl.BlockSpec((None, None, LANES, width), lambda b, g, i: (g, b * nq + i, 0, 0)),
                  pl.BlockSpec((None, seq_len, LANES), lambda b, g, i: (g, b, 0)),
                  pl.BlockSpec((None, None, n_kv, VT_ROWS, TKV), lambda b, g, i: (b, g, 0, 0, 0))],
        out_specs=pl.BlockSpec((TQ, GROUP * HEAD_DIM), lambda b, g, i: (b * nq + i, g)),
        out_shape=jax.ShapeDtypeStruct((n, W_QA), BF16),
        scratch_shapes=[pltpu.VMEM((TKV, width), F32), pltpu.VMEM((TKV, width), BF16),
                        pltpu.VMEM((VT_ROWS, width), F32), pltpu.VMEM((1, width), F32)],
        compiler_params=_cparams(3),
        name="attn",
    )(shift, q, k, vt)


def _decay_tables(dec_ref, h):
    lg = jax.nn.log_sigmoid(dec_ref[h])
    return jnp.broadcast_to(lg[0:1, :], (RET_CHUNK, LANES))


def _ret_fwd_kernel(q_ref, k_ref, v_ref, decf_ref, decb_ref, y_ref, r_scr):
    ts = v_ref.shape[0]
    c_len = RET_CHUNK
    n_c = ts // c_len

    @pl.when(pl.program_id(1) == 0)
    def _():
        r_scr[...] = jnp.zeros_like(r_scr)

    ri = lax.broadcasted_iota(jnp.int32, (c_len, LANES), 0).astype(F32)
    ci = lax.broadcasted_iota(jnp.int32, (c_len, LANES), 1).astype(F32)
    diff = ri - ci
    dn_t = (((1,), (1,)), ((), ()))
    dn_kv = (((0,), (0,)), ((), ()))
    blk = lambda h, c: (slice(c * c_len, (c + 1) * c_len), slice(h * RET_DV, (h + 1) * RET_DV))
    kv = {}
    for h in range(H_RET):
        lf = _decay_tables(decf_ref, h)
        lb = _decay_tables(decb_ref, h)
        dmat = jnp.where(diff >= 0, jnp.exp(lf * jnp.maximum(diff, 0.0)), jnp.exp(lb * jnp.maximum(-diff, 0.0)))
        kdec = jnp.exp(lf * (c_len - 1.0 - ri))
        for c in range(n_c):
            rows, cols = blk(h, c)
            qh = q_ref[h, rows, :]
            kh = k_ref[h, rows, :]
            vh = v_ref[rows, cols]
            s = lax.dot_general(qh, kh, dn_t, preferred_element_type=F32)
            y_ref[rows, cols] = jnp.dot((s * dmat).astype(BF16), vh, preferred_element_type=F32)
            vd = (vh.astype(F32) * kdec).astype(BF16)
            kv[h, c] = lax.dot_general(kh, vd, dn_kv, preferred_element_type=F32)
    for h in range(H_RET):
        lf = _decay_tables(decf_ref, h)
        qdec = jnp.exp(lf * (ri + 1.0))
        cdec = jnp.exp(lf * float(c_len))
        r = r_scr[h]
        for c in range(n_c):
            rows, cols = blk(h, c)
            y_ref[rows, cols] += jnp.dot(q_ref[h, rows, :], r.astype(BF16), preferred_element_type=F32) * qdec
            r = r * cdec + kv[h, c]
        r_scr[h] = r


def _ret_bwd_kernel(q_ref, k_ref, v_ref, y_ref, gr_ref, decb_ref, o_ref, r_scr):
    ts = v_ref.shape[0]
    c_len = RET_CHUNK
    n_c = ts // c_len

    @pl.when(pl.program_id(1) == 0)
    def _():
        r_scr[...] = jnp.zeros_like(r_scr)

    ri = lax.broadcasted_iota(jnp.int32, (c_len, LANES), 0).astype(F32)
    dn_kv = (((0,), (0,)), ((), ()))
    blk = lambda h, c: (slice(c * c_len, (c + 1) * c_len), slice(h * RET_DV, (h + 1) * RET_DV))
    kv = {}
    for h in range(H_RET):
        kdec = jnp.exp(_decay_tables(decb_ref, h) * ri)
        for c in range(n_c):
            rows, cols = blk(h, c)
            vd = (v_ref[rows, cols].astype(F32) * kdec).astype(BF16)
            kv[h, c] = lax.dot_general(k_ref[h, rows, :], vd, dn_kv, preferred_element_type=F32)
    for h in range(H_RET):
        lb = _decay_tables(decb_ref, h)
        qdec = jnp.exp(lb * (float(c_len) - ri))
        cdec = jnp.exp(lb * float(c_len))
        r = r_scr[h]
        for c in reversed(range(n_c)):
            rows, cols = blk(h, c)
            y = y_ref[rows, cols] + jnp.dot(q_ref[h, rows, :], r.astype(BF16), preferred_element_type=F32) * qdec
            yn = y * lax.rsqrt(jnp.mean(y * y, axis=-1, keepdims=True) + EPS)
            o_ref[rows, cols] = (gr_ref[rows, cols].astype(F32) * yn).astype(o_ref.dtype)
            r = r * cdec + kv[h, c]
        r_scr[h] = r


def _retention(qr, kr, vr, gr, decf, decb, seq_len):
    n = vr.shape[0]
    n_b = n // seq_len
    ts = TS_RET
    nt = seq_len // ts
    fwd = lambda b, j: b * nt + j
    bwd = lambda b, j: b * nt + (nt - 1 - j)
    heads = lambda f: pl.BlockSpec((H_RET, ts, LANES), lambda b, j: (0, f(b, j), 0))
    row = lambda f: pl.BlockSpec((ts, W_VR), lambda b, j: (f(b, j), 0))
    dec_spec = _const_spec((H_RET, 8, LANES))
    state = pltpu.VMEM((H_RET, LANES, RET_DV), F32)
    y1 = pl.pallas_call(
        _ret_fwd_kernel,
        grid=(n_b, nt),
        in_specs=[heads(fwd), heads(fwd), row(fwd), dec_spec, dec_spec],
        out_specs=row(fwd),
        out_shape=jax.ShapeDtypeStruct((n, W_VR), F32),
        scratch_shapes=[state],
        compiler_params=_cparams(2),
        name="ret_fwd",
    )(qr, kr, vr, decf, decb)
    return pl.pallas_call(
        _ret_bwd_kernel,
        grid=(n_b, nt),
        in_specs=[heads(bwd), heads(bwd), row(bwd), row(bwd), row(bwd), dec_spec],
        out_specs=row(bwd),
        out_shape=jax.ShapeDtypeStruct((n, W_VR), BF16),
        scratch_shapes=[state],
        compiler_params=_cparams(2),
        name="ret_bwd",
    )(qr, kr, vr, y1, gr, decb)


def _merge_kernel(x_ref, a_ref, r_ref, gate_ref, wa_ref, wr_ref, wo_ref, o_ref):
    ad = jnp.dot(a_ref[...], wa_ref[...], preferred_element_type=F32)
    rd = jnp.dot(r_ref[...], wr_ref[...], preferred_element_type=F32)
    g_a = gate_ref[:, :D_MODEL].astype(F32)
    g_r = gate_ref[:, D_MODEL:].astype(F32)
    m = (g_a * ad + g_r * rd).astype(BF16)
    o_ref[...] = x_ref[...] + jnp.dot(m, wo_ref[...], preferred_element_type=F32)


def _merge(x2, attn, ret, gates, wa, wr, wo):
    n = x2.shape[0]
    tm = TM_PROJ
    row = lambda width: pl.BlockSpec((tm, width), lambda i: (i, 0))
    return pl.pallas_call(
        _merge_kernel,
        grid=(n // tm,),
        in_specs=[row(D_MODEL), row(W_QA), row(W_VR), row(W_GATE),
                  _const_spec((W_QA, D_MODEL)), _const_spec((W_VR, D_MODEL)), _const_spec((D_MODEL, D_MODEL))],
        out_specs=row(D_MODEL),
        out_shape=jax.ShapeDtypeStruct((n, D_MODEL), F32),
        compiler_params=_cparams(1),
        name="merge",
    )(x2, attn, ret, gates, wa, wr, wo)


def _ffn_kernel(xp_ref, x_ref, xn_ref, g2_ref, wup_ref, cw_ref, cb_ref, wdn_ref, o_ref, act_scr, *, seq_len):
    tm = x_ref.shape[0]
    rows = tm + 2 * HALO
    i = pl.program_id(0)
    first = (i * tm) % seq_len == 0
    last = ((i + 1) * tm) % seq_len == 0

    def norm(x):
        return x * lax.rsqrt(jnp.mean(x * x, axis=-1, keepdims=True) + EPS) * g2_ref[...]

    hp = jnp.where(first, 0.0, norm(xp_ref[...]))
    hn = jnp.where(last, 0.0, norm(xn_ref[...]))
    x = x_ref[...]
    h = jnp.concatenate([hp, norm(x), hn], axis=0).astype(BF16)

    def conv(c0):
        u = jnp.dot(h, wup_ref[:, c0:c0 + FC], preferred_element_type=F32)
        w = cw_ref[:, c0:c0 + FC]
        down = pltpu.roll(u, 1, 0)[HALO:HALO + tm]
        up = pltpu.roll(u, rows - 1, 0)[HALO:HALO + tm]
        return down * w[0:1] + u[HALO:HALO + tm] * w[1:2] + up * w[2:3] + cb_ref[:, c0:c0 + FC]

    for f in range(D_FF // FC):
        c0 = f * FC
        val = conv(c0)
        gt = conv(D_FF + c0)
        gelu = 0.5 * gt * (1.0 + lax.erf(gt * (1.0 / math.sqrt(2.0))))
        act_scr[:, c0:c0 + FC] = (gelu * val).astype(BF16)
    o_ref[...] = x + jnp.dot(act_scr[...], wdn_ref[...], preferred_element_type=F32)


def _ffn(x2, seq_len, norm_g, w_up, conv_w, conv_b, w_down):
    n = x2.shape[0]
    tm = TM_PROJ
    hb = tm // HALO
    n_hb = n // HALO
    return pl.pallas_call(
        functools.partial(_ffn_kernel, seq_len=seq_len),
        grid=(n // tm,),
        in_specs=[pl.BlockSpec((HALO, D_MODEL), lambda i: (jnp.maximum(i * hb - 1, 0), 0)),
                  pl.BlockSpec((tm, D_MODEL), lambda i: (i, 0)),
                  pl.BlockSpec((HALO, D_MODEL), lambda i: (jnp.minimum((i + 1) * hb, n_hb - 1), 0)),
                  _const_spec((1, D_MODEL)), _const_spec((D_MODEL, 2 * D_FF)),
                  _const_spec((3, 2 * D_FF)), _const_spec((1, 2 * D_FF)), _const_spec((D_FF, D_MODEL))],
        out_specs=pl.BlockSpec((tm, D_MODEL), lambda i: (i, 0)),
        out_shape=jax.ShapeDtypeStruct((n, D_MODEL), F32),
        scratch_shapes=[pltpu.VMEM((tm, D_FF), BF16)],
        compiler_params=_cparams(1),
        name="ffn",
    )(x2, x2, x2, norm_g, w_up, conv_w, conv_b, w_down)


def _layer(x2, seq_len, tables, gmat, p):
    q, k, vt, qr, kr, vr, gr, gates = _in_proj(x2, seq_len, p["norm1_g"], p["w_in"], p["qg"], p["kg"],
                                               p["shift_row"], p["b_gate"], tables, gmat)
    attn = _attention(p["shift"], q, k, vt, seq_len)
    ret = _retention(qr, kr, vr, gr, p["decf"], p["decb"], seq_len)
    x2 = _merge(x2, attn, ret, gates, p["w_attn_o"], p["w_ret_o"], p["w_out"])
    return _ffn(x2, seq_len, p["norm2_g"], p["w_up"], p["conv_w"], p["conv_b"], p["w_down"])


def _layer_params(l, norm1_g, w_in, q_norm_g, k_norm_g, ret_decay_fwd, ret_decay_bwd, w_attn_o, w_ret_o,
                  b_gate, w_out, norm2_g, w_up, conv_w, conv_b, w_down):
    head_order = [hh for b in range(GROUP) for hh in (b, GROUP + b)]
    q_cols = jnp.asarray([hh * HEAD_DIM + d for hh in head_order for d in range(HEAD_DIM)], jnp.int32)
    w = w_in[l]
    w_p = jnp.concatenate([jnp.take(w[:, :W_QA], q_cols, axis=1), w[:, W_QA:]], axis=1).astype(BF16)
    two = lambda g: jnp.concatenate([g, g]).reshape(1, LANES)
    bcast = lambda d: jnp.broadcast_to(d.astype(F32)[:, None, None], (H_RET, 8, LANES))
    shift = (BOUND_MARGIN * HEAD_DIM * LOG2E / math.sqrt(HEAD_DIM)) * (
        jnp.max(jnp.abs(q_norm_g[l])) * jnp.max(jnp.abs(k_norm_g[l]))).astype(F32)
    return dict(
        norm1_g=norm1_g[l].reshape(1, D_MODEL), w_in=w_p,
        qg=two(q_norm_g[l]) * (LOG2E / math.sqrt(HEAD_DIM)), kg=two(k_norm_g[l]),
        shift=shift.reshape(1), shift_row=jnp.broadcast_to(shift, (1, LANES)),
        decf=bcast(ret_decay_fwd[l]), decb=bcast(ret_decay_bwd[l]),
        w_attn_o=w_attn_o[l].astype(BF16), w_ret_o=w_ret_o[l].astype(BF16),
        b_gate=b_gate[l].reshape(1, W_GATE), w_out=w_out[l].astype(BF16),
        norm2_g=norm2_g[l].reshape(1, D_MODEL), w_up=w_up[l].astype(BF16),
        conv_w=conv_w[l], conv_b=conv_b[l].reshape(1, 2 * D_FF), w_down=w_down[l].astype(BF16))


def kernel(x_prompt, x_sample, norm1_g, w_in, q_norm_g, k_norm_g, ret_decay_fwd, ret_decay_bwd, w_attn_o, w_ret_o, b_gate, w_out, norm2_g, w_up, conv_w, conv_b, w_down):
    weights = (norm1_g, w_in, q_norm_g, k_norm_g, ret_decay_fwd, ret_decay_bwd, w_attn_o, w_ret_o,
               b_gate, w_out, norm2_g, w_up, conv_w, conv_b, w_down)
    depth = w_in.shape[0]
    lane = jnp.arange(2 * LANES) % LANES
    gmat = ((lane[:, None] // HEAD_DIM) == (jnp.arange(LANES)[None, :] // HEAD_DIM)).astype(BF16) / HEAD_DIM
    outs = []
    for x in (x_prompt, x_sample):
        n_b, seq_len, _ = x.shape
        tables = _rope_tables(seq_len)
        x2 = x.reshape(n_b * seq_len, D_MODEL)
        for l in range(depth):
            x2 = _layer(x2, seq_len, tables, gmat, _layer_params(l, *weights))
        outs.append(x2.reshape(n_b, seq_len, D_MODEL))
    return tuple(outs)
```
